```python
import math
import jax, jax.numpy as jnp
from jax import lax
import numpy as np


D_MODEL = 1024
BATCH = 2
SEQ = 8192
DEPTH = 4

N_HEADS = 8
KV_HEADS = 2
GROUP = N_HEADS // KV_HEADS
HEAD_DIM = 64
Q_DIM = N_HEADS * HEAD_DIM
KV_DIM = KV_HEADS * HEAD_DIM
WINDOW = 128
BLOCK = 128
NUM_BUCKETS = 32
MAX_DISTANCE = 128
CONV_DIM = 512
CONV_WIDTH = 31
N_BRANCHES = 2
GATE_DIM = N_BRANCHES * D_MODEL
IN_DIM = Q_DIM + 2 * KV_DIM + 2 * CONV_DIM + GATE_DIM
D_FF = 2816
PLE_DIM = 256
NEG_INF = -1e9

kernel_name = "hybrid_gated_swa_conformer_encoder"


def rms_norm(x, g, eps=1e-6):
    xf = x.astype(jnp.float32)
    y = xf * lax.rsqrt(jnp.mean(xf * xf, axis=-1, keepdims=True) + eps)
    return (y * g.astype(jnp.float32)).astype(x.dtype)


def layer_norm(x, g, b, eps=1e-5):
    xf = x.astype(jnp.float32)
    mu = jnp.mean(xf, axis=-1, keepdims=True)
    var = jnp.mean(jnp.square(xf - mu), axis=-1, keepdims=True)
    y = (xf - mu) * lax.rsqrt(var + eps)
    return (y * g.astype(jnp.float32) + b.astype(jnp.float32)).astype(x.dtype)


def swiglu(x, w_in, w_out):
    gate, up = jnp.split(x @ w_in, 2, axis=-1)
    return (jax.nn.silu(gate) * up) @ w_out


def t5_buckets(rel):
    half = NUM_BUCKETS // 2
    max_exact = half // 2
    n = jnp.abs(rel)
    ret = jnp.where(rel > 0, half, 0)
    nf = jnp.maximum(n, 1).astype(jnp.float32)
    large = max_exact + (jnp.log(nf / max_exact) / math.log(MAX_DISTANCE / max_exact)
                         * (half - max_exact)).astype(jnp.int32)
    large = jnp.minimum(large, half - 1)
    return ret + jnp.where(n < max_exact, n, large)


def window_attention(q, k, v, sink, rel_bias):
    B, S = q.shape[0], q.shape[1]
    nb = S // BLOCK
    qb = q.reshape(B, nb, BLOCK, KV_HEADS, GROUP, HEAD_DIM)

    def windows(t):
        tp = jnp.pad(t, ((0, 0), (BLOCK, BLOCK), (0, 0), (0, 0)))
        tp = tp.reshape(B, nb + 2, BLOCK, KV_HEADS, HEAD_DIM)
        return jnp.concatenate([tp[:, :-2], tp[:, 1:-1], tp[:, 2:]], axis=2)

    kw, vw = windows(k), windows(v)
    scale = HEAD_DIM ** -0.5
    s = jnp.einsum('bnqkgd,bnskd->bnkgqs', qb, kw).astype(jnp.float32) * scale

    qi = jnp.arange(BLOCK)
    kj = jnp.arange(3 * BLOCK)
    rel = kj[None, :] - BLOCK - qi[:, None]
    bias = rel_bias.astype(jnp.float32)[t5_buckets(rel)]
    bias = bias.transpose(2, 0, 1).reshape(KV_HEADS, GROUP, BLOCK, 3 * BLOCK)
    kpos = jnp.arange(nb)[:, None] * BLOCK - BLOCK + kj[None, :]
    valid = (jnp.abs(rel) <= WINDOW)[None] & ((kpos >= 0) & (kpos < S))[:, None, :]

    s = s + bias[None, None]
    s = jnp.where(valid[None, :, None, None], s, NEG_INF)
    sink_l = jnp.broadcast_to(sink.astype(jnp.float32).reshape(KV_HEADS, GROUP)[None, None, :, :, None, None],
                              s.shape[:-1] + (1,))
    pr = jax.nn.softmax(jnp.concatenate([s, sink_l], axis=-1), axis=-1)[..., :-1]
    o = jnp.einsum('bnkgqs,bnskd->bnqkgd', pr.astype(v.dtype), vw)
    return o.reshape(B, S, Q_DIM)


def depthwise_conv(c, w, b):
    pad = CONV_WIDTH // 2
    y = lax.conv_general_dilated(c, w.reshape(CONV_WIDTH, 1, CONV_DIM).astype(c.dtype),
                                 window_strides=(1,), padding=[(pad, pad)],
                                 dimension_numbers=('NWC', 'WIO', 'NWC'),
                                 feature_group_count=CONV_DIM)
    return y + b


def setup_inputs(seed: int = 0) -> dict:
    key = jax.random.key(seed)
    ks = jax.random.split(key, 26)
    f32 = jnp.float32

    def w(k, shape, fan_in):
        return jax.random.normal(k, shape, f32) * (fan_in ** -0.5)

    def gain(k, shape):
        return 1.0 + 0.02 * jax.random.normal(k, shape, f32)

    L, D = DEPTH, D_MODEL
    return {
        "x": jax.random.normal(ks[0], (BATCH, SEQ, D), f32),
        "p": jax.random.normal(ks[1], (DEPTH, BATCH, SEQ, PLE_DIM), f32),
        "rel_bias": 0.1 * jax.random.normal(ks[2], (NUM_BUCKETS, N_HEADS), f32),
        "norm_ffn1": gain(ks[3], (L, D)),
        "w_ffn1_in": w(ks[4], (L, D, 2 * D_FF), D),
        "w_ffn1_out": w(ks[5], (L, D_FF, D), D_FF),
        "norm_mix": gain(ks[6], (L, D)),
        "w_in": w(ks[7], (L, D, IN_DIM), D),
        "q_norm": gain(ks[8], (L, HEAD_DIM)),
        "k_norm": gain(ks[9], (L, HEAD_DIM)),
        "sink": 0.5 * jax.random.normal(ks[10], (L, N_HEADS), f32),
        "conv_w": w(ks[11], (L, CONV_WIDTH, CONV_DIM), CONV_WIDTH),
        "conv_b": 0.02 * jax.random.normal(ks[12], (L, CONV_DIM), f32),
        "conv_ln_g": gain(ks[13], (L, CONV_DIM)),
        "conv_ln_b": 0.02 * jax.random.normal(ks[14], (L, CONV_DIM), f32),
        "w_attn_out": w(ks[15], (L, Q_DIM, D), Q_DIM),
        "w_conv_out": w(ks[16], (L, CONV_DIM, D), CONV_DIM),
        "w_o": w(ks[17], (L, D, D), D),
        "norm_ffn2": gain(ks[18], (L, D)),
        "w_ffn2_in": w(ks[19], (L, D, 2 * D_FF), D),
        "w_ffn2_out": w(ks[20], (L, D_FF, D), D_FF),
        "norm_pe": gain(ks[21], (L, D)),
        "w_pe_gate": w(ks[22], (L, D, D), D),
        "w_pe_proj": w(ks[23], (L, PLE_DIM, D), PLE_DIM),
    }


def reference(x, p, rel_bias, norm_ffn1, w_ffn1_in, w_ffn1_out, norm_mix, w_in, q_norm, k_norm,
              sink, conv_w, conv_b, conv_ln_g, conv_ln_b, w_attn_out, w_conv_out, w_o,
              norm_ffn2, w_ffn2_in, w_ffn2_out, norm_pe, w_pe_gate, w_pe_proj):
    B, S = x.shape[0], x.shape[1]
    splits = [Q_DIM, Q_DIM + KV_DIM, Q_DIM + 2 * KV_DIM, Q_DIM + 2 * KV_DIM + 2 * CONV_DIM]
    for i in range(DEPTH):
        h = x + 0.5 * swiglu(rms_norm(x, norm_ffn1[i]), w_ffn1_in[i], w_ffn1_out[i])

        u = rms_norm(h, norm_mix[i])
        q, k, v, c, g = jnp.split(u @ w_in[i], splits, axis=-1)

        q = rms_norm(q.reshape(B, S, N_HEADS, HEAD_DIM), q_norm[i])
        k = rms_norm(k.reshape(B, S, KV_HEADS, HEAD_DIM), k_norm[i])
        v = v.reshape(B, S, KV_HEADS, HEAD_DIM)
        y_attn = window_attention(q, k, v, sink[i], rel_bias) @ w_attn_out[i]

        c_val, c_gate = jnp.split(c, 2, axis=-1)
        c = c_val * jax.nn.sigmoid(c_gate)
        c = depthwise_conv(c, conv_w[i], conv_b[i])
        c = jax.nn.silu(layer_norm(c, conv_ln_g[i], conv_ln_b[i]))
        y_conv = c @ w_conv_out[i]

        g_attn, g_conv = jnp.split(jax.nn.sigmoid(g), 2, axis=-1)
        h = h + (g_attn * y_attn + g_conv * y_conv) @ w_o[i]

        h = h + 0.5 * swiglu(rms_norm(h, norm_ffn2[i]), w_ffn2_in[i], w_ffn2_out[i])

        x = h + (p[i] @ w_pe_proj[i]) * jax.nn.sigmoid(rms_norm(h, norm_pe[i]) @ w_pe_gate[i])
    return x
```

```python
import functools
import math

import numpy as np
import jax
import jax.numpy as jnp
from jax import lax
from jax.experimental import pallas as pl
from jax.experimental.pallas import tpu as pltpu

N_HEADS = 8
KV_HEADS = 2
GROUP = N_HEADS // KV_HEADS
HEAD_DIM = 64
Q_DIM = N_HEADS * HEAD_DIM
KV_DIM = KV_HEADS * HEAD_DIM
BLOCK = 128
NUM_BUCKETS = 32
MAX_DISTANCE = 128
CONV_DIM = 512
CONV_WIDTH = 31
CONV_PAD = CONV_WIDTH // 2
CONV_HALO = 16
NEG_INF = -1e9

TILE_M = 512
FF_CHUNK = 256
CONV_ROWS = 64
LANES = 128
VMEM_LIMIT_BYTES = 56 * 1024 * 1024

F32 = jnp.float32
BF16 = jnp.bfloat16


def _rms(x, g, eps=1e-6):
    return x * lax.rsqrt(jnp.mean(x * x, axis=-1, keepdims=True) + eps) * g


def _dot(a, b):
    return jnp.dot(a, b, preferred_element_type=F32)


def _group_sumsq(x, ones_blockdiag):
    x2 = x * x
    hi = x2.astype(BF16)
    lo = (x2 - hi.astype(F32)).astype(BF16)
    return _dot(hi, ones_blockdiag) + _dot(lo, ones_blockdiag)


def _swiglu_half_step(x, g_ref, win_ref, wout_ref, act_ref):
    d_ff = wout_ref.shape[0]
    xn = _rms(x, g_ref[...]).astype(BF16)
    for lo in range(0, d_ff, FF_CHUNK):
        gate = _dot(xn, win_ref[:, lo:lo + FF_CHUNK])
        up = _dot(xn, win_ref[:, d_ff + lo:d_ff + lo + FF_CHUNK])
        act_ref[:, lo:lo + FF_CHUNK] = (gate * jax.nn.sigmoid(gate) * up).astype(BF16)
    return x + 0.5 * _dot(act_ref[...], wout_ref[...])


def _ffn_kvc_kernel(x_ref, g1_ref, win_ref, wout_ref, gmix_ref, wkvc_ref, kg_ref, ones_ref,
                    h_ref, k_ref, v_ref, c_ref, act_ref):
    h = _swiglu_half_step(x_ref[...], g1_ref, win_ref, wout_ref, act_ref)
    h_ref[...] = h
    u = _rms(h, gmix_ref[...]).astype(BF16)
    kvc = _dot(u, wkvc_ref[...])
    k = kvc[:, :KV_DIM]
    ss = _group_sumsq(k, ones_ref[...])
    k_ref[...] = (k * lax.rsqrt(ss * (1.0 / HEAD_DIM) + 1e-6) * kg_ref[...]).astype(BF16)
    v_ref[...] = kvc[:, KV_DIM:2 * KV_DIM].astype(BF16)
    c_val = kvc[:, 2 * KV_DIM:2 * KV_DIM + CONV_DIM]
    c_gate = kvc[:, 2 * KV_DIM + CONV_DIM:]
    c_ref[...] = c_val * jax.nn.sigmoid(c_gate)


def _ffn_ple_kernel(x_ref, g1_ref, win_ref, wout_ref, gpe_ref, wpg_ref, p_ref, wpp_ref,
                    o_ref, act_ref):
    h = _swiglu_half_step(x_ref[...], g1_ref, win_ref, wout_ref, act_ref)
    t = _rms(h, gpe_ref[...]).astype(BF16)
    gate = jax.nn.sigmoid(_dot(t, wpg_ref[...]))
    pe = _dot(p_ref[...].astype(BF16), wpp_ref[...])
    o_ref[...] = h + pe * gate


def _mix_kernel(tiles_per_seq,
                h_ref, kp_ref, kc_ref, kn_ref, vp_ref, vc_ref, vn_ref, cp_ref, cc_ref, cn_ref,
                gmix_ref, wqg_ref, qg_ref, ones_ref, bias_ref, sink_ref,
                cw_ref, cb_ref, lng_ref, lnb_ref, wao_ref, wco_ref, wo_ref,
                o_ref, kext_ref, vext_ref, cext_ref, y_ref, conv_ref):
    tm = h_ref.shape[0]
    n_blocks = tm // BLOCK
    tile = pl.program_id(0) % tiles_per_seq
    is_first = tile == 0
    is_last = tile == tiles_per_seq - 1

    h = h_ref[...]
    u = _rms(h, gmix_ref[...]).astype(BF16)
    qg = _dot(u, wqg_ref[...])
    q = qg[:, :Q_DIM]
    ss = _group_sumsq(q, ones_ref[...])
    qn = (q * lax.rsqrt(ss * (1.0 / HEAD_DIM) + 1e-6) * qg_ref[...]).astype(BF16)

    kext_ref[:BLOCK] = kp_ref[...]
    kext_ref[BLOCK:BLOCK + tm] = kc_ref[...]
    kext_ref[BLOCK + tm:] = kn_ref[...]
    vext_ref[:BLOCK] = vp_ref[...]
    vext_ref[BLOCK:BLOCK + tm] = vc_ref[...]
    vext_ref[BLOCK + tm:] = vn_ref[...]

    for r in range(n_blocks):
        if r == 0:
            table = jnp.where(is_first, 0, 1)
        elif r == n_blocks - 1:
            table = jnp.where(is_last, 2, 1)
        else:
            table = 1
        rows = slice(r * BLOCK, (r + 1) * BLOCK)
        win = slice(r * BLOCK, (r + 3) * BLOCK)
        for kv in range(KV_HEADS):
            lanes = slice(kv * HEAD_DIM, (kv + 1) * HEAD_DIM)
            q4 = jnp.concatenate(
                [qn[rows, (kv * GROUP + j) * HEAD_DIM:(kv * GROUP + j + 1) * HEAD_DIM]
                 for j in range(GROUP)], axis=0)
            kw = kext_ref[win, lanes]
            vw = vext_ref[win, lanes]
            s = lax.dot_general(q4, kw, (((1,), (1,)), ((), ())),
                                preferred_element_type=F32) + bias_ref[table, kv]
            sink = sink_ref[kv]
            m = jnp.maximum(jnp.max(s, axis=-1, keepdims=True), sink)
            e = jnp.exp(s - m)
            denom = jnp.sum(e, axis=-1, keepdims=True) + jnp.exp(sink - m)
            o = _dot(e.astype(BF16), vw) / denom
            for j in range(GROUP):
                head = kv * GROUP + j
                y_ref[rows, head * HEAD_DIM:(head + 1) * HEAD_DIM] = (
                    o[j * BLOCK:(j + 1) * BLOCK].astype(BF16))
    y_attn = _dot(y_ref[...], wao_ref[...])

    cext_ref[:CONV_HALO] = jnp.where(is_first, 0.0, cp_ref[...])
    cext_ref[CONV_HALO:CONV_HALO + tm] = cc_ref[...]
    cext_ref[CONV_HALO + tm:] = jnp.where(is_last, 0.0, cn_ref[...])
    shift = CONV_HALO - CONV_PAD
    for r0 in range(0, tm, CONV_ROWS):
        for l0 in range(0, CONV_DIM, LANES):
            acc = jnp.zeros((CONV_ROWS, LANES), F32)
            for w in range(CONV_WIDTH):
                acc = acc + (cext_ref[r0 + shift + w:r0 + shift + w + CONV_ROWS, l0:l0 + LANES]
                             * cw_ref[w:w + 1, l0:l0 + LANES])
            conv_ref[r0:r0 + CONV_ROWS, l0:l0 + LANES] = acc + cb_ref[:, l0:l0 + LANES]
    c = conv_ref[...]
    mu = jnp.mean(c, axis=-1, keepdims=True)
    var = jnp.mean(jnp.square(c - mu), axis=-1, keepdims=True)
    c = (c - mu) * lax.rsqrt(var + 1e-5) * lng_ref[...] + lnb_ref[...]
    y_conv = _dot((c * jax.nn.sigmoid(c)).astype(BF16), wco_ref[...])

    gates = jax.nn.sigmoid(qg[:, Q_DIM:])
    d = h.shape[1]
    merged = gates[:, :d] * y_attn + gates[:, d:] * y_conv
    o_ref[...] = h + _dot(merged.astype(BF16), wo_ref[...])


def _t5_buckets():
    half = NUM_BUCKETS // 2
    max_exact = half // 2
    rel = np.arange(3 * BLOCK)[None, :] - BLOCK - np.arange(BLOCK)[:, None]
    n = np.abs(rel)
    nf = np.maximum(n, 1).astype(np.float64)
    large = max_exact + (np.log(nf / max_exact) / math.log(MAX_DISTANCE / max_exact)
                         * (half - max_exact)).astype(np.int32)
    large = np.minimum(large, half - 1)
    return rel, np.where(rel > 0, half, 0) + np.where(n < max_exact, n, large)


def _bias_tables(rel_bias):
    rel, buckets = _t5_buckets()
    bias = rel_bias.astype(F32)[buckets]
    bias = bias.transpose(2, 0, 1).reshape(KV_HEADS, GROUP * BLOCK, 3 * BLOCK)
    band = np.abs(rel) <= BLOCK
    col = np.arange(3 * BLOCK)[None, :]
    masks = [band & (col >= BLOCK), band, band & (col < 2 * BLOCK)]
    masks = np.stack([np.tile(m, (GROUP, 1)) for m in masks])
    return jnp.where(masks[:, None], bias[None], NEG_INF)


def _blockdiag_ones(n):
    idx = np.arange(n) // HEAD_DIM
    return jnp.asarray(idx[:, None] == idx[None, :], dtype=BF16)


def _const(shape):
    zeros = (0,) * len(shape)
    return pl.BlockSpec(shape, lambda i: zeros, pipeline_mode=pl.Buffered(1))


def _rows(width, tm=TILE_M):
    return pl.BlockSpec((tm, width), lambda i: (i, 0))


_PARAMS = pltpu.CompilerParams(dimension_semantics=("arbitrary",),
                               vmem_limit_bytes=VMEM_LIMIT_BYTES)


def _ffn_kvc(x, g1, win, wout, gmix, wkvc, kg, ones):
    m, d = x.shape
    d_ff = wout.shape[0]
    return pl.pallas_call(
        _ffn_kvc_kernel,
        grid=(m // TILE_M,),
        in_specs=[_rows(d), _const((1, d)), _const(win.shape), _const(wout.shape),
                  _const((1, d)), _const(wkvc.shape), _const((1, KV_DIM)), _const(ones.shape)],
        out_specs=[_rows(d), _rows(KV_DIM), _rows(KV_DIM), _rows(CONV_DIM)],
        out_shape=[jax.ShapeDtypeStruct((m, d), F32), jax.ShapeDtypeStruct((m, KV_DIM), BF16),
                   jax.ShapeDtypeStruct((m, KV_DIM), BF16), jax.ShapeDtypeStruct((m, CONV_DIM), F32)],
        scratch_shapes=[pltpu.VMEM((TILE_M, d_ff), BF16)],
        compiler_params=_PARAMS,
        name="ffn_kvc",
    )(x, g1, win, wout, gmix, wkvc, kg, ones)


def _ffn_ple(x, g1, win, wout, gpe, wpg, p, wpp):
    m, d = x.shape
    d_ff = wout.shape[0]
    return pl.pallas_call(
        _ffn_ple_kernel,
        grid=(m // TILE_M,),
        in_specs=[_rows(d), _const((1, d)), _const(win.shape), _const(wout.shape),
                  _const((1, d)), _const(wpg.shape), _rows(p.shape[1]), _const(wpp.shape)],
        out_specs=_rows(d),
        out_shape=jax.ShapeDtypeStruct((m, d), F32),
        scratch_shapes=[pltpu.VMEM((TILE_M, d_ff), BF16)],
        compiler_params=_PARAMS,
        name="ffn_ple",
    )(x, g1, win, wout, gpe, wpg, p, wpp)


def _mix(h, k, v, c, seq, gmix, wqg, qgain, ones, bias, sink, cw, cb, lng, lnb, wao, wco, wo):
    m, d = h.shape
    tm = TILE_M
    tiles_per_seq = seq // tm
    kv_blocks = tm // BLOCK
    c_blocks = tm // CONV_HALO
    last_kv = m // BLOCK - 1
    last_c = m // CONV_HALO - 1

    def halo(rows, width, stride, last, side):
        if side < 0:
            return pl.BlockSpec((rows, width), lambda i: (jnp.maximum(i * stride - 1, 0), 0))
        return pl.BlockSpec((rows, width), lambda i: (jnp.minimum((i + 1) * stride, last), 0))

    kv_specs = [halo(BLOCK, KV_DIM, kv_blocks, last_kv, -1), _rows(KV_DIM),
                halo(BLOCK, KV_DIM, kv_blocks, last_kv, +1)]
    c_specs = [halo(CONV_HALO, CONV_DIM, c_blocks, last_c, -1), _rows(CONV_DIM),
               halo(CONV_HALO, CONV_DIM, c_blocks, last_c, +1)]
    return pl.pallas_call(
        functools.partial(_mix_kernel, tiles_per_seq),
        grid=(m // tm,),
        in_specs=[_rows(d)] + kv_specs + kv_specs + c_specs + [
            _const((1, d)), _const(wqg.shape), _const((1, Q_DIM)), _const(ones.shape),
            _const(bias.shape), _const(sink.shape),
            _const(cw.shape), _const((1, CONV_DIM)), _const((1, CONV_DIM)), _const((1, CONV_DIM)),
            _const(wao.shape), _const(wco.shape), _const(wo.shape)],
        out_specs=_rows(d),
        out_shape=jax.ShapeDtypeStruct((m, d), F32),
        scratch_shapes=[pltpu.VMEM((tm + 2 * BLOCK, KV_DIM), BF16),
                        pltpu.VMEM((tm + 2 * BLOCK, KV_DIM), BF16),
                        pltpu.VMEM((tm + 2 * CONV_HALO, CONV_DIM), F32),
                        pltpu.VMEM((tm, Q_DIM), BF16),
                        pltpu.VMEM((tm, CONV_DIM), F32)],
        compiler_params=_PARAMS,
        name="mix",
    )(h, k, k, k, v, v, v, c, c, c, gmix, wqg, qgain, ones, bias, sink, cw, cb, lng, lnb,
      wao, wco, wo)


def kernel(x, p, rel_bias, norm_ffn1, w_ffn1_in, w_ffn1_out, norm_mix, w_in, q_norm, k_norm, sink, conv_w, conv_b, conv_ln_g, conv_ln_b, w_attn_out, w_conv_out, w_o, norm_ffn2, w_ffn2_in, w_ffn2_out, norm_pe, w_pe_gate, w_pe_proj):
    batch, seq, d = x.shape
    depth = p.shape[0]
    m = batch * seq
    assert seq % TILE_M == 0 and TILE_M % BLOCK == 0 and seq // BLOCK >= 2

    kvc_lo, kvc_hi = Q_DIM, Q_DIM + 2 * KV_DIM + 2 * CONV_DIM
    w_kvc = w_in[:, :, kvc_lo:kvc_hi].astype(BF16)
    w_qg = jnp.concatenate([w_in[:, :, :kvc_lo], w_in[:, :, kvc_hi:]], axis=-1).astype(BF16)
    w1_in, w1_out = w_ffn1_in.astype(BF16), w_ffn1_out.astype(BF16)
    w2_in, w2_out = w_ffn2_in.astype(BF16), w_ffn2_out.astype(BF16)
    w_ao, w_co, w_oo = w_attn_out.astype(BF16), w_conv_out.astype(BF16), w_o.astype(BF16)
    w_pg, w_pp = w_pe_gate.astype(BF16), w_pe_proj.astype(BF16)

    bias = _bias_tables(rel_bias)
    ones_q, ones_k = _blockdiag_ones(Q_DIM), _blockdiag_ones(KV_DIM)
    q_gain = jnp.tile(q_norm, (1, N_HEADS)) * (HEAD_DIM ** -0.5)
    k_gain = jnp.tile(k_norm, (1, KV_HEADS))
    sink_rows = jnp.repeat(sink.reshape(depth, KV_HEADS, GROUP), BLOCK, axis=-1)[..., None]

    row = lambda a, i: a[i][None, :]
    xf = x.reshape(m, d)
    pf = p.reshape(depth, m, p.shape[-1])
    for i in range(depth):
        h, k, v, c = _ffn_kvc(xf, row(norm_ffn1, i), w1_in[i], w1_out[i], row(norm_mix, i),
                              w_kvc[i], row(k_gain, i), ones_k)
        h = _mix(h, k, v, c, seq, row(norm_mix, i), w_qg[i], row(q_gain, i), ones_q, bias,
                 sink_rows[i], conv_w[i], row(conv_b, i), row(conv_ln_g, i), row(conv_ln_b, i),
                 w_ao[i], w_co[i], w_oo[i])
        xf = _ffn_ple(h, row(norm_ffn2, i), w2_in[i], w2_out[i], row(norm_pe, i), w_pg[i],
                      pf[i], w_pp[i])
    return xf.reshape(batch, seq, d)
```

```python
import functools
import math

import numpy as np
import jax
import jax.numpy as jnp
from jax import lax
from jax.experimental import pallas as pl
from jax.experimental.pallas import tpu as pltpu

N_HEADS = 8
KV_HEADS = 2
GROUP = N_HEADS // KV_HEADS
HEAD_DIM = 64
Q_DIM = N_HEADS * HEAD_DIM
KV_DIM = KV_HEADS * HEAD_DIM
BLOCK = 128
NUM_BUCKETS = 32
MAX_DISTANCE = 128
CONV_DIM = 512
CONV_WIDTH = 31
CONV_PAD = CONV_WIDTH // 2
CONV_HALO = 16
NEG_INF = -1e9

TILE_M = 512
FF_CHUNK = 256
CONV_ROWS = 64
LANES = 128
SUBLANES = 8
VMEM_LIMIT_BYTES = 56 * 1024 * 1024

F32 = jnp.float32
BF16 = jnp.bfloat16


def _rms(x, g, eps=1e-6):
    return x * lax.rsqrt(jnp.mean(x * x, axis=-1, keepdims=True) + eps) * g


def _dot(a, b):
    return jnp.dot(a, b, preferred_element_type=F32)


def _group_sumsq(x, ones_blockdiag):
    x2 = x * x
    hi = x2.astype(BF16)
    lo = (x2 - hi.astype(F32)).astype(BF16)
    return _dot(hi, ones_blockdiag) + _dot(lo, ones_blockdiag)


def _swiglu_half_step(x, g_ref, win_ref, wout_ref, act_ref):
    d_ff = wout_ref.shape[0]
    xn = _rms(x, g_ref[...]).astype(BF16)
    for lo in range(0, d_ff, FF_CHUNK):
        gate = _dot(xn, win_ref[:, lo:lo + FF_CHUNK])
        up = _dot(xn, win_ref[:, d_ff + lo:d_ff + lo + FF_CHUNK])
        act_ref[:, lo:lo + FF_CHUNK] = (gate * jax.nn.sigmoid(gate) * up).astype(BF16)
    return x + 0.5 * _dot(act_ref[...], wout_ref[...])


def _ffn_kvc_kernel(x_ref, g1_ref, win_ref, wout_ref, gmix_ref, wkvc_ref, kg_ref, ones_ref,
                    h_ref, k_ref, v_ref, c_ref, act_ref):
    h = _swiglu_half_step(x_ref[...], g1_ref, win_ref, wout_ref, act_ref)
    h_ref[...] = h
    u = _rms(h, gmix_ref[...]).astype(BF16)
    kvc = _dot(u, wkvc_ref[...])
    k = kvc[:, :KV_DIM]
    ss = _group_sumsq(k, ones_ref[...])
    k_ref[...] = (k * lax.rsqrt(ss * (1.0 / HEAD_DIM) + 1e-6) * kg_ref[...]).astype(BF16)
    v_ref[...] = kvc[:, KV_DIM:2 * KV_DIM].astype(BF16)
    c_val = kvc[:, 2 * KV_DIM:2 * KV_DIM + CONV_DIM]
    c_gate = kvc[:, 2 * KV_DIM + CONV_DIM:]
    c_ref[...] = c_val * jax.nn.sigmoid(c_gate)


def _ffn_ple_kernel(x_ref, g1_ref, win_ref, wout_ref, gpe_ref, wpg_ref, p_ref, wpp_ref,
                    o_ref, act_ref):
    h = _swiglu_half_step(x_ref[...], g1_ref, win_ref, wout_ref, act_ref)
    t = _rms(h, gpe_ref[...]).astype(BF16)
    gate = jax.nn.sigmoid(_dot(t, wpg_ref[...]))
    pe = _dot(p_ref[...].astype(BF16), wpp_ref[...])
    o_ref[...] = h + pe * gate


def _mix_kernel(tiles_per_seq,
                h_ref, kp_ref, kc_ref, kn_ref, vp_ref, vc_ref, vn_ref, cp_ref, cc_ref, cn_ref,
                gmix_ref, wq_ref, wg_ref, qgain_ref, ones_ref, bias_ref, sink_ref,
                cw_ref, cb_ref, lng_ref, lnb_ref, wao_ref, wco_ref, wo_ref,
                o_ref, kext_ref, vext_ref, cext_ref, cshift_ref, y_ref, cact_ref, gate_ref):
    tm, d = h_ref.shape
    n_blocks = tm // BLOCK
    tile = pl.program_id(0) % tiles_per_seq
    is_first = tile == 0
    is_last = tile == tiles_per_seq - 1

    h = h_ref[...]
    u = _rms(h, gmix_ref[...]).astype(BF16)
    q = _dot(u, wq_ref[...])
    ss = _group_sumsq(q, ones_ref[...])
    qn = (q * lax.rsqrt(ss * (1.0 / HEAD_DIM) + 1e-6) * qgain_ref[...]).astype(BF16)

    cext_ref[:CONV_HALO] = jnp.where(is_first, 0.0, cp_ref[...])
    cext_ref[CONV_HALO:CONV_HALO + tm] = cc_ref[...]
    cext_ref[CONV_HALO + tm:] = jnp.where(is_last, 0.0, cn_ref[...])
    span = cshift_ref.shape[1]
    for s in range(1, SUBLANES):
        cshift_ref[s - 1] = cext_ref[s:s + span, :]

    n_chunks = tm // CONV_ROWS
    gate_cols = gate_ref.shape[1] // n_chunks
    for ci in range(n_chunks):
        r0 = ci * CONV_ROWS
        accs = []
        for l0 in range(0, CONV_DIM, LANES):
            acc = jnp.zeros((CONV_ROWS, LANES), F32)
            for w in range(CONV_WIDTH):
                off = CONV_HALO - CONV_PAD + w
                base = r0 + off - off % SUBLANES
                if off % SUBLANES == 0:
                    win = cext_ref[base:base + CONV_ROWS, l0:l0 + LANES]
                else:
                    win = cshift_ref[off % SUBLANES - 1, base:base + CONV_ROWS, l0:l0 + LANES]
                acc = acc + win * cw_ref[w:w + 1, l0:l0 + LANES]
            accs.append(acc)
        c = jnp.concatenate(accs, axis=1) + cb_ref[...]
        mu = jnp.mean(c, axis=-1, keepdims=True)
        var = jnp.mean(jnp.square(c - mu), axis=-1, keepdims=True)
        c = (c - mu) * lax.rsqrt(var + 1e-5) * lng_ref[...] + lnb_ref[...]
        cact_ref[r0:r0 + CONV_ROWS, :] = (c * jax.nn.sigmoid(c)).astype(BF16)
        g0 = ci * gate_cols
        gate_ref[:, g0:g0 + gate_cols] = jax.nn.sigmoid(_dot(u, wg_ref[:, g0:g0 + gate_cols]))
    y_conv = _dot(cact_ref[...], wco_ref[...])

    kext_ref[:BLOCK] = kp_ref[...]
    kext_ref[BLOCK:BLOCK + tm] = kc_ref[...]
    kext_ref[BLOCK + tm:] = kn_ref[...]
    vext_ref[:BLOCK] = vp_ref[...]
    vext_ref[BLOCK:BLOCK + tm] = vc_ref[...]
    vext_ref[BLOCK + tm:] = vn_ref[...]

    for r in range(n_blocks):
        if r == 0:
            table = jnp.where(is_first, 0, 1)
        elif r == n_blocks - 1:
            table = jnp.where(is_last, 2, 1)
        else:
            table = 1
        rows = slice(r * BLOCK, (r + 1) * BLOCK)
        win = slice(r * BLOCK, (r + 3) * BLOCK)
        for kv in range(KV_HEADS):
            lanes = slice(kv * HEAD_DIM, (kv + 1) * HEAD_DIM)
            q4 = jnp.concatenate(
                [qn[rows, (kv * GROUP + j) * HEAD_DIM:(kv * GROUP + j + 1) * HEAD_DIM]
                 for j in range(GROUP)], axis=0)
            kw = kext_ref[win, lanes]
            vw = vext_ref[win, lanes]
            s = lax.dot_general(q4, kw, (((1,), (1,)), ((), ())),
                                preferred_element_type=F32) + bias_ref[table, kv]
            m = jnp.max(s, axis=-1, keepdims=True)
            e = jnp.exp(s - m)
            denom = jnp.sum(e, axis=-1, keepdims=True) + jnp.exp(sink_ref[kv] - m)
            o = _dot(e.astype(BF16), vw) / denom[:, :HEAD_DIM]
            for j in range(GROUP):
                head = kv * GROUP + j
                y_ref[rows, head * HEAD_DIM:(head + 1) * HEAD_DIM] = (
                    o[j * BLOCK:(j + 1) * BLOCK].astype(BF16))
    y_attn = _dot(y_ref[...], wao_ref[...])

    merged = gate_ref[:, :d] * y_attn + gate_ref[:, d:] * y_conv
    o_ref[...] = h + _dot(merged.astype(BF16), wo_ref[...])


def _bias_tables(rel_bias):
    half = NUM_BUCKETS // 2
    max_exact = half // 2
    period = 4 * BLOCK
    k = np.arange(period)
    rel = np.where(k < 3 * BLOCK, k - BLOCK, k - BLOCK - period)
    n = np.abs(rel)
    nf = np.maximum(n, 1).astype(np.float64)
    large = max_exact + (np.log(nf / max_exact) / math.log(MAX_DISTANCE / max_exact)
                         * (half - max_exact)).astype(np.int32)
    buckets = np.where(rel > 0, half, 0) + np.where(n < max_exact, n, np.minimum(large, half - 1))
    vec = jnp.where((n <= BLOCK)[None, :], rel_bias.astype(F32)[buckets].T, NEG_INF)
    skew = jnp.tile(vec, (1, BLOCK))[:, :BLOCK * (period - 1)].reshape(N_HEADS, BLOCK, period - 1)
    bias = skew[:, :, :3 * BLOCK].reshape(KV_HEADS, GROUP * BLOCK, 3 * BLOCK)
    col = np.arange(3 * BLOCK)
    edge = np.stack([col >= BLOCK, col >= 0, col < 2 * BLOCK])
    return jnp.where(edge[:, None, None, :], bias[None], NEG_INF)


def _blockdiag_ones(n):
    idx = np.arange(n) // HEAD_DIM
    return jnp.asarray(idx[:, None] == idx[None, :], dtype=BF16)


def _const(shape):
    zeros = (0,) * len(shape)
    return pl.BlockSpec(shape, lambda i: zeros, pipeline_mode=pl.Buffered(1))


def _rows(width, tm=TILE_M):
    return pl.BlockSpec((tm, width), lambda i: (i, 0))


_PARAMS = pltpu.CompilerParams(dimension_semantics=("arbitrary",),
                               vmem_limit_bytes=VMEM_LIMIT_BYTES)


def _ffn_kvc(x, g1, win, wout, gmix, wkvc, kg, ones):
    m, d = x.shape
    d_ff = wout.shape[0]
    return pl.pallas_call(
        _ffn_kvc_kernel,
        grid=(m // TILE_M,),
        in_specs=[_rows(d), _const((1, d)), _const(win.shape), _const(wout.shape),
                  _const((1, d)), _const(wkvc.shape), _const((1, KV_DIM)), _const(ones.shape)],
        out_specs=[_rows(d), _rows(KV_DIM), _rows(KV_DIM), _rows(CONV_DIM)],
        out_shape=[jax.ShapeDtypeStruct((m, d), F32), jax.ShapeDtypeStruct((m, KV_DIM), BF16),
                   jax.ShapeDtypeStruct((m, KV_DIM), BF16), jax.ShapeDtypeStruct((m, CONV_DIM), F32)],
        scratch_shapes=[pltpu.VMEM((TILE_M, d_ff), BF16)],
        compiler_params=_PARAMS,
        name="ffn_kvc",
    )(x, g1, win, wout, gmix, wkvc, kg, ones)


def _ffn_ple(x, g1, win, wout, gpe, wpg, p, wpp):
    m, d = x.shape
    d_ff = wout.shape[0]
    return pl.pallas_call(
        _ffn_ple_kernel,
        grid=(m // TILE_M,),
        in_specs=[_rows(d), _const((1, d)), _const(win.shape), _const(wout.shape),
                  _const((1, d)), _const(wpg.shape), _rows(p.shape[1]), _const(wpp.shape)],
        out_specs=_rows(d),
        out_shape=jax.ShapeDtypeStruct((m, d), F32),
        scratch_shapes=[pltpu.VMEM((TILE_M, d_ff), BF16)],
        compiler_params=_PARAMS,
        name="ffn_ple",
    )(x, g1, win, wout, gpe, wpg, p, wpp)


def _mix(h, k, v, c, seq, gmix, wq, wg, qgain, ones, bias, sink, cw, cb, lng, lnb, wao, wco, wo):
    m, d = h.shape
    tm = TILE_M
    tiles_per_seq = seq // tm
    kv_blocks = tm // BLOCK
    c_blocks = tm // CONV_HALO
    last_kv = m // BLOCK - 1
    last_c = m // CONV_HALO - 1

    def halo(rows, width, stride, last, side):
        if side < 0:
            return pl.BlockSpec((rows, width), lambda i: (jnp.maximum(i * stride - 1, 0), 0))
        return pl.BlockSpec((rows, width), lambda i: (jnp.minimum((i + 1) * stride, last), 0))

    kv_specs = [halo(BLOCK, KV_DIM, kv_blocks, last_kv, -1), _rows(KV_DIM),
                halo(BLOCK, KV_DIM, kv_blocks, last_kv, +1)]
    c_specs = [halo(CONV_HALO, CONV_DIM, c_blocks, last_c, -1), _rows(CONV_DIM),
               halo(CONV_HALO, CONV_DIM, c_blocks, last_c, +1)]
    shift_rows = tm + (CONV_HALO - CONV_PAD + CONV_WIDTH - 1) // SUBLANES * SUBLANES
    return pl.pallas_call(
        functools.partial(_mix_kernel, tiles_per_seq),
        grid=(m // tm,),
        in_specs=[_rows(d)] + kv_specs + kv_specs + c_specs + [
            _const((1, d)), _const(wq.shape), _const(wg.shape), _const((1, Q_DIM)),
            _const(ones.shape), _const(bias.shape), _const(sink.shape),
            _const(cw.shape), _const((1, CONV_DIM)), _const((1, CONV_DIM)), _const((1, CONV_DIM)),
            _const(wao.shape), _const(wco.shape), _const(wo.shape)],
        out_specs=_rows(d),
        out_shape=jax.ShapeDtypeStruct((m, d), F32),
        scratch_shapes=[pltpu.VMEM((tm + 2 * BLOCK, KV_DIM), BF16),
                        pltpu.VMEM((tm + 2 * BLOCK, KV_DIM), BF16),
                        pltpu.VMEM((tm + 2 * CONV_HALO, CONV_DIM), F32),
                        pltpu.VMEM((SUBLANES - 1, shift_rows, CONV_DIM), F32),
                        pltpu.VMEM((tm, Q_DIM), BF16),
                        pltpu.VMEM((tm, CONV_DIM), BF16),
                        pltpu.VMEM((tm, 2 * d), F32)],
        compiler_params=_PARAMS,
        name="mix",
    )(h, k, k, k, v, v, v, c, c, c, gmix, wq, wg, qgain, ones, bias, sink, cw, cb, lng, lnb,
      wao, wco, wo)


def kernel(x, p, rel_bias, norm_ffn1, w_ffn1_in, w_ffn1_out, norm_mix, w_in, q_norm, k_norm, sink, conv_w, conv_b, conv_ln_g, conv_ln_b, w_attn_out, w_conv_out, w_o, norm_ffn2, w_ffn2_in, w_ffn2_out, norm_pe, w_pe_gate, w_pe_proj):
    batch, seq, d = x.shape
    depth = p.shape[0]
    m = batch * seq
    assert seq % TILE_M == 0 and TILE_M % BLOCK == 0 and seq // BLOCK >= 2

    kvc_lo, kvc_hi = Q_DIM, Q_DIM + 2 * KV_DIM + 2 * CONV_DIM
    w_q = w_in[:, :, :kvc_lo].astype(BF16)
    w_kvc = w_in[:, :, kvc_lo:kvc_hi].astype(BF16)
    w_g = w_in[:, :, kvc_hi:].astype(BF16)
    w1_in, w1_out = w_ffn1_in.astype(BF16), w_ffn1_out.astype(BF16)
    w2_in, w2_out = w_ffn2_in.astype(BF16), w_ffn2_out.astype(BF16)
    w_ao, w_co, w_oo = w_attn_out.astype(BF16), w_conv_out.astype(BF16), w_o.astype(BF16)
    w_pg, w_pp = w_pe_gate.astype(BF16), w_pe_proj.astype(BF16)

    bias = _bias_tables(rel_bias)
    ones_q, ones_k = _blockdiag_ones(Q_DIM), _blockdiag_ones(KV_DIM)
    q_gain = jnp.tile(q_norm, (1, N_HEADS)) * (HEAD_DIM ** -0.5)
    k_gain = jnp.tile(k_norm, (1, KV_HEADS))
    sink_rows = jnp.repeat(sink.reshape(depth, KV_HEADS, GROUP), BLOCK, axis=-1)
    sink_rows = jnp.broadcast_to(sink_rows[..., None], sink_rows.shape + (LANES,))

    row = lambda a, i: a[i][None, :]
    xf = x.reshape(m, d)
    pf = p.reshape(depth, m, p.shape[-1])
    for i in range(depth):
        h, k, v, c = _ffn_kvc(xf, row(norm_ffn1, i), w1_in[i], w1_out[i], row(norm_mix, i),
                              w_kvc[i], row(k_gain, i), ones_k)
        h = _mix(h, k, v, c, seq, row(norm_mix, i), w_q[i], w_g[i], row(q_gain, i), ones_q, bias,
                 sink_rows[i], conv_w[i], row(conv_b, i), row(conv_ln_g, i), row(conv_ln_b, i),
                 w_ao[i], w_co[i], w_oo[i])
        xf = _ffn_ple(h, row(norm_ffn2, i), w2_in[i], w2_out[i], row(norm_pe, i), w_pg[i],
                      pf[i], w_pp[i])
    return xf.reshape(batch, seq, d)
```

```python
import functools
import math

import numpy as np
import jax
import jax.numpy as jnp
from jax import lax
from jax.experimental import pallas as pl
from jax.experimental.pallas import tpu as pltpu

N_HEADS = 8
KV_HEADS = 2
GROUP = N_HEADS // KV_HEADS
HEAD_DIM = 64
Q_DIM = N_HEADS * HEAD_DIM
KV_DIM = KV_HEADS * HEAD_DIM
BLOCK = 128
NUM_BUCKETS = 32
MAX_DISTANCE = 128
CONV_DIM = 512
CONV_WIDTH = 31
CONV_PAD = CONV_WIDTH // 2
CONV_HALO = 16
NEG_INF = -1e9

TILE_M = 512
FF_CHUNK = 256
CONV_ROWS = 64
LANES = 128
SUBLANES = 8
VMEM_LIMIT_BYTES = 56 * 1024 * 1024

F32 = jnp.float32
BF16 = jnp.bfloat16


def _rms(x, g, eps=1e-6):
    return x * lax.rsqrt(jnp.mean(x * x, axis=-1, keepdims=True) + eps) * g


def _dot(a, b):
    return jnp.dot(a, b, preferred_element_type=F32)


def _group_sumsq(x, ones_blockdiag):
    x2 = x * x
    hi = x2.astype(BF16)
    lo = (x2 - hi.astype(F32)).astype(BF16)
    return _dot(hi, ones_blockdiag) + _dot(lo, ones_blockdiag)


def _swiglu_half_step(x, g_ref, win_ref, wout_ref, act_ref):
    d_ff = wout_ref.shape[0]
    xn = _rms(x, g_ref[...]).astype(BF16)
    for lo in range(0, d_ff, FF_CHUNK):
        gate = _dot(xn, win_ref[:, lo:lo + FF_CHUNK])
        up = _dot(xn, win_ref[:, d_ff + lo:d_ff + lo + FF_CHUNK])
        act_ref[:, lo:lo + FF_CHUNK] = (gate * jax.nn.sigmoid(gate) * up).astype(BF16)
    return x + 0.5 * _dot(act_ref[...], wout_ref[...])


def _ffn_kvc_kernel(x_ref, g1_ref, win_ref, wout_ref, gmix_ref, wkvc_ref, kg_ref, ones_ref,
                    h_ref, k_ref, v_ref, c_ref, act_ref):
    h = _swiglu_half_step(x_ref[...], g1_ref, win_ref, wout_ref, act_ref)
    h_ref[...] = h
    u = _rms(h, gmix_ref[...]).astype(BF16)
    kvc = _dot(u, wkvc_ref[...])
    k = kvc[:, :KV_DIM]
    ss = _group_sumsq(k, ones_ref[...])
    k_ref[...] = (k * lax.rsqrt(ss * (1.0 / HEAD_DIM) + 1e-6) * kg_ref[...]).astype(BF16)
    v_ref[...] = kvc[:, KV_DIM:2 * KV_DIM].astype(BF16)
    c_val = kvc[:, 2 * KV_DIM:2 * KV_DIM + CONV_DIM]
    c_gate = kvc[:, 2 * KV_DIM + CONV_DIM:]
    c_ref[...] = c_val * jax.nn.sigmoid(c_gate)


def _ffn_ple_kernel(x_ref, g1_ref, win_ref, wout_ref, gpe_ref, wpg_ref, p_ref, wpp_ref,
                    o_ref, act_ref):
    h = _swiglu_half_step(x_ref[...], g1_ref, win_ref, wout_ref, act_ref)
    t = _rms(h, gpe_ref[...]).astype(BF16)
    gate = jax.nn.sigmoid(_dot(t, wpg_ref[...]))
    pe = _dot(p_ref[...].astype(BF16), wpp_ref[...])
    o_ref[...] = h + pe * gate


def _mix_kernel(tiles_per_seq,
                h_ref, kp_ref, kc_ref, kn_ref, vp_ref, vc_ref, vn_ref, cp_ref, cc_ref, cn_ref,
                gmix_ref, wq_ref, wg_ref, qgain_ref, ones_ref, bias_ref, sink_ref,
                cw_ref, cb_ref, lng_ref, lnb_ref, wao_ref, wco_ref, wo_ref,
                o_ref, kext_ref, vext_ref, cext_ref, cshift_ref, y_ref, cact_ref, gate_ref):
    tm, d = h_ref.shape
    n_blocks = tm // BLOCK
    tile = pl.program_id(0) % tiles_per_seq
    is_first = tile == 0
    is_last = tile == tiles_per_seq - 1

    h = h_ref[...]
    u = _rms(h, gmix_ref[...]).astype(BF16)
    q = _dot(u, wq_ref[...])
    ss = _group_sumsq(q, ones_ref[...])
    qn = (q * lax.rsqrt(ss * (1.0 / HEAD_DIM) + 1e-6) * qgain_ref[...]).astype(BF16)

    cext_ref[:CONV_HALO] = jnp.where(is_first, 0.0, cp_ref[...])
    cext_ref[CONV_HALO:CONV_HALO + tm] = cc_ref[...]
    cext_ref[CONV_HALO + tm:] = jnp.where(is_last, 0.0, cn_ref[...])
    span = cshift_ref.shape[1]
    for s in range(1, SUBLANES):
        cshift_ref[s - 1] = cext_ref[s:s + span, :]

    n_chunks = tm // CONV_ROWS
    gate_cols = gate_ref.shape[1] // n_chunks
    for ci in range(n_chunks):
        r0 = ci * CONV_ROWS
        accs = []
        for l0 in range(0, CONV_DIM, LANES):
            acc = jnp.zeros((CONV_ROWS, LANES), F32)
            for w in range(CONV_WIDTH):
                off = CONV_HALO - CONV_PAD + w
                base = r0 + off - off % SUBLANES
                if off % SUBLANES == 0:
                    win = cext_ref[base:base + CONV_ROWS, l0:l0 + LANES]
                else:
                    win = cshift_ref[off % SUBLANES - 1, base:base + CONV_ROWS, l0:l0 + LANES]
                acc = acc + win * cw_ref[w:w + 1, l0:l0 + LANES]
            accs.append(acc)
        c = jnp.concatenate(accs, axis=1) + cb_ref[...]
        mu = jnp.mean(c, axis=-1, keepdims=True)
        var = jnp.mean(jnp.square(c - mu), axis=-1, keepdims=True)
        c = (c - mu) * lax.rsqrt(var + 1e-5) * lng_ref[...] + lnb_ref[...]
        cact_ref[r0:r0 + CONV_ROWS, :] = (c * jax.nn.sigmoid(c)).astype(BF16)
        g0 = ci * gate_cols
        gate_ref[:, g0:g0 + gate_cols] = jax.nn.sigmoid(_dot(u, wg_ref[:, g0:g0 + gate_cols]))
    y_conv = _dot(cact_ref[...], wco_ref[...])

    kext_ref[:BLOCK] = kp_ref[...]
    kext_ref[BLOCK:BLOCK + tm] = kc_ref[...]
    kext_ref[BLOCK + tm:] = kn_ref[...]
    vext_ref[:BLOCK] = vp_ref[...]
    vext_ref[BLOCK:BLOCK + tm] = vc_ref[...]
    vext_ref[BLOCK + tm:] = vn_ref[...]

    for r in range(n_blocks):
        if r == 0:
            table = jnp.where(is_first, 0, 1)
        elif r == n_blocks - 1:
            table = jnp.where(is_last, 2, 1)
        else:
            table = 1
        rows = slice(r * BLOCK, (r + 1) * BLOCK)
        win = slice(r * BLOCK, (r + 3) * BLOCK)
        for kv in range(KV_HEADS):
            lanes = slice(kv * HEAD_DIM, (kv + 1) * HEAD_DIM)
            q4 = jnp.concatenate(
                [qn[rows, (kv * GROUP + j) * HEAD_DIM:(kv * GROUP + j + 1) * HEAD_DIM]
                 for j in range(GROUP)], axis=0)
            kw = kext_ref[win, lanes]
            vw = vext_ref[win, lanes]
            s = lax.dot_general(q4, kw, (((1,), (1,)), ((), ())),
                                preferred_element_type=F32) + bias_ref[table, kv]
            m = jnp.max(s, axis=-1, keepdims=True)
            e = jnp.exp(s - m)
            denom = jnp.sum(e, axis=-1, keepdims=True) + jnp.exp(sink_ref[kv] - m)
            o = _dot(e.astype(BF16), vw) / denom[:, :HEAD_DIM]
            for j in range(GROUP):
                head = kv * GROUP + j
                y_ref[rows, head * HEAD_DIM:(head + 1) * HEAD_DIM] = (
                    o[j * BLOCK:(j + 1) * BLOCK].astype(BF16))
    y_attn = _dot(y_ref[...], wao_ref[...])

    merged = gate_ref[:, :d] * y_attn + gate_ref[:, d:] * y_conv
    o_ref[...] = h + _dot(merged.astype(BF16), wo_ref[...])


def _bias_tables(rel_bias):
    half = NUM_BUCKETS // 2
    max_exact = half // 2
    period = 4 * BLOCK
    k = np.arange(period)
    rel = np.where(k < 3 * BLOCK, k - BLOCK, k - BLOCK - period)
    n = np.abs(rel)
    nf = np.maximum(n, 1).astype(np.float64)
    large = max_exact + (np.log(nf / max_exact) / math.log(MAX_DISTANCE / max_exact)
                         * (half - max_exact)).astype(np.int32)
    buckets = np.where(rel > 0, half, 0) + np.where(n < max_exact, n, np.minimum(large, half - 1))
    vec = jnp.where((n <= BLOCK)[None, :], rel_bias.astype(F32)[buckets].T, NEG_INF)
    skew = jnp.tile(vec, (1, BLOCK))[:, :BLOCK * (period - 1)].reshape(N_HEADS, BLOCK, period - 1)
    bias = skew[:, :, :3 * BLOCK].reshape(KV_HEADS, GROUP * BLOCK, 3 * BLOCK)
    col = np.arange(3 * BLOCK)
    edge = np.stack([col >= BLOCK, col >= 0, col < 2 * BLOCK])
    return jnp.where(edge[:, None, None, :], bias[None], NEG_INF)


def _blockdiag_ones(n):
    idx = np.arange(n) // HEAD_DIM
    return jnp.asarray(idx[:, None] == idx[None, :], dtype=BF16)


def _whole(arr):
    zeros = (0,) * arr.ndim
    return pl.BlockSpec(arr.shape, lambda i: zeros, pipeline_mode=pl.Buffered(1))


def _layer(arr, layer):
    index = (layer,) + (0,) * (arr.ndim - 1)
    return pl.BlockSpec((None,) + arr.shape[1:], lambda i: index, pipeline_mode=pl.Buffered(1))


def _rows(width, tm=TILE_M):
    return pl.BlockSpec((tm, width), lambda i: (i, 0))


_PARAMS = pltpu.CompilerParams(dimension_semantics=("arbitrary",),
                               vmem_limit_bytes=VMEM_LIMIT_BYTES)


def _ffn_kvc(layer, x, g1, win, wout, gmix, wkvc, kg, ones):
    m, d = x.shape
    d_ff = wout.shape[1]
    params = (g1, win, wout, gmix, wkvc, kg)
    return pl.pallas_call(
        _ffn_kvc_kernel,
        grid=(m // TILE_M,),
        in_specs=[_rows(d)] + [_layer(a, layer) for a in params] + [_whole(ones)],
        out_specs=[_rows(d), _rows(KV_DIM), _rows(KV_DIM), _rows(CONV_DIM)],
        out_shape=[jax.ShapeDtypeStruct((m, d), F32), jax.ShapeDtypeStruct((m, KV_DIM), BF16),
                   jax.ShapeDtypeStruct((m, KV_DIM), BF16), jax.ShapeDtypeStruct((m, CONV_DIM), F32)],
        scratch_shapes=[pltpu.VMEM((TILE_M, d_ff), BF16)],
        compiler_params=_PARAMS,
        name="ffn_kvc",
    )(x, *params, ones)


def _ffn_ple(layer, x, g1, win, wout, gpe, wpg, p, wpp):
    m, d = x.shape
    d_ff = wout.shape[1]
    p_spec = pl.BlockSpec((None, TILE_M, p.shape[2]), lambda i: (layer, i, 0))
    return pl.pallas_call(
        _ffn_ple_kernel,
        grid=(m // TILE_M,),
        in_specs=[_rows(d)] + [_layer(a, layer) for a in (g1, win, wout, gpe, wpg)]
                 + [p_spec, _layer(wpp, layer)],
        out_specs=_rows(d),
        out_shape=jax.ShapeDtypeStruct((m, d), F32),
        scratch_shapes=[pltpu.VMEM((TILE_M, d_ff), BF16)],
        compiler_params=_PARAMS,
        name="ffn_ple",
    )(x, g1, win, wout, gpe, wpg, p, wpp)


def _mix(layer, h, k, v, c, seq, gmix, wq, wg, qgain, ones, bias, sink, cw, cb, lng, lnb,
         wao, wco, wo):
    m, d = h.shape
    tm = TILE_M
    tiles_per_seq = seq // tm
    kv_blocks = tm // BLOCK
    c_blocks = tm // CONV_HALO
    last_kv = m // BLOCK - 1
    last_c = m // CONV_HALO - 1

    def halo(rows, width, stride, last, side):
        if side < 0:
            return pl.BlockSpec((rows, width), lambda i: (jnp.maximum(i * stride - 1, 0), 0))
        return pl.BlockSpec((rows, width), lambda i: (jnp.minimum((i + 1) * stride, last), 0))

    kv_specs = [halo(BLOCK, KV_DIM, kv_blocks, last_kv, -1), _rows(KV_DIM),
                halo(BLOCK, KV_DIM, kv_blocks, last_kv, +1)]
    c_specs = [halo(CONV_HALO, CONV_DIM, c_blocks, last_c, -1), _rows(CONV_DIM),
               halo(CONV_HALO, CONV_DIM, c_blocks, last_c, +1)]
    shift_rows = tm + (CONV_HALO - CONV_PAD + CONV_WIDTH - 1) // SUBLANES * SUBLANES
    lay = lambda a: _layer(a, layer)
    return pl.pallas_call(
        functools.partial(_mix_kernel, tiles_per_seq),
        grid=(m // tm,),
        in_specs=[_rows(d)] + kv_specs + kv_specs + c_specs + [
            lay(gmix), lay(wq), lay(wg), lay(qgain), _whole(ones), _whole(bias), lay(sink),
            lay(cw), lay(cb), lay(lng), lay(lnb), lay(wao), lay(wco), lay(wo)],
        out_specs=_rows(d),
        out_shape=jax.ShapeDtypeStruct((m, d), F32),
        scratch_shapes=[pltpu.VMEM((tm + 2 * BLOCK, KV_DIM), BF16),
                        pltpu.VMEM((tm + 2 * BLOCK, KV_DIM), BF16),
                        pltpu.VMEM((tm + 2 * CONV_HALO, CONV_DIM), F32),
                        pltpu.VMEM((SUBLANES - 1, shift_rows, CONV_DIM), F32),
                        pltpu.VMEM((tm, Q_DIM), BF16),
                        pltpu.VMEM((tm, CONV_DIM), BF16),
                        pltpu.VMEM((tm, 2 * d), F32)],
        compiler_params=_PARAMS,
        name="mix",
    )(h, k, k, k, v, v, v, c, c, c, gmix, wq, wg, qgain, ones, bias, sink, cw, cb, lng, lnb,
      wao, wco, wo)


def kernel(x, p, rel_bias, norm_ffn1, w_ffn1_in, w_ffn1_out, norm_mix, w_in, q_norm, k_norm, sink, conv_w, conv_b, conv_ln_g, conv_ln_b, w_attn_out, w_conv_out, w_o, norm_ffn2, w_ffn2_in, w_ffn2_out, norm_pe, w_pe_gate, w_pe_proj):
    batch, seq, d = x.shape
    depth = p.shape[0]
    m = batch * seq
    assert seq % TILE_M == 0 and TILE_M % BLOCK == 0 and seq // BLOCK >= 2

    kvc_lo, kvc_hi = Q_DIM, Q_DIM + 2 * KV_DIM + 2 * CONV_DIM
    w_q = w_in[:, :, :kvc_lo].astype(BF16)
    w_kvc = w_in[:, :, kvc_lo:kvc_hi].astype(BF16)
    w_g = w_in[:, :, kvc_hi:].astype(BF16)
    w1_in, w1_out = w_ffn1_in.astype(BF16), w_ffn1_out.astype(BF16)
    w2_in, w2_out = w_ffn2_in.astype(BF16), w_ffn2_out.astype(BF16)
    w_ao, w_co, w_oo = w_attn_out.astype(BF16), w_conv_out.astype(BF16), w_o.astype(BF16)
    w_pg, w_pp = w_pe_gate.astype(BF16), w_pe_proj.astype(BF16)

    bias = _bias_tables(rel_bias)
    ones_q, ones_k = _blockdiag_ones(Q_DIM), _blockdiag_ones(KV_DIM)
    vec = lambda a: a[:, None, :]
    q_gain = vec(jnp.tile(q_norm, (1, N_HEADS)) * (HEAD_DIM ** -0.5))
    k_gain = vec(jnp.tile(k_norm, (1, KV_HEADS)))
    sink_rows = jnp.repeat(sink.reshape(depth, KV_HEADS, GROUP), BLOCK, axis=-1)
    sink_rows = jnp.broadcast_to(sink_rows[..., None], sink_rows.shape + (LANES,))
    g_ffn1, g_mix, g_ffn2, g_pe = vec(norm_ffn1), vec(norm_mix), vec(norm_ffn2), vec(norm_pe)
    cb, lng, lnb = vec(conv_b), vec(conv_ln_g), vec(conv_ln_b)

    xf = x.reshape(m, d)
    pf = p.reshape(depth, m, p.shape[-1])
    for i in range(depth):
        h, k, v, c = _ffn_kvc(i, xf, g_ffn1, w1_in, w1_out, g_mix, w_kvc, k_gain, ones_k)
        h = _mix(i, h, k, v, c, seq, g_mix, w_q, w_g, q_gain, ones_q, bias, sink_rows,
                 conv_w, cb, lng, lnb, w_ao, w_co, w_oo)
        xf = _ffn_ple(i, h, g_ffn2, w2_in, w2_out, g_pe, w_pg, pf, w_pp)
    return xf.reshape(batch, seq, d)
```

```python
import functools
import math

import numpy as np
import jax
import jax.numpy as jnp
from jax import lax
from jax.experimental import pallas as pl
from jax.experimental.pallas import tpu as pltpu

N_HEADS = 8
KV_HEADS = 2
GROUP = N_HEADS // KV_HEADS
HEAD_DIM = 64
Q_DIM = N_HEADS * HEAD_DIM
KV_DIM = KV_HEADS * HEAD_DIM
BLOCK = 128
NUM_BUCKETS = 32
MAX_DISTANCE = 128
CONV_DIM = 512
CONV_WIDTH = 31
CONV_PAD = CONV_WIDTH // 2
CONV_HALO = 16
NEG_INF = -1e9
LOG2_E = math.log2(math.e)

TILE_M = 512
FF_CHUNK = 256
CONV_ROWS = 64
LANES = 128
SUBLANES = 8
VMEM_LIMIT_BYTES = 56 * 1024 * 1024

F32 = jnp.float32
BF16 = jnp.bfloat16


def _rms(x, g, eps=1e-6):
    return x * lax.rsqrt(jnp.mean(x * x, axis=-1, keepdims=True) + eps) * g


def _dot(a, b):
    return jnp.dot(a, b, preferred_element_type=F32)


def _group_sumsq(x, ones_blockdiag):
    x2 = x * x
    hi = x2.astype(BF16)
    lo = (x2 - hi.astype(F32)).astype(BF16)
    return _dot(hi, ones_blockdiag) + _dot(lo, ones_blockdiag)


def _swiglu_half_step(x, g_ref, win_ref, wout_ref, act_ref):
    d_ff = wout_ref.shape[0]
    xn = _rms(x, g_ref[...]).astype(BF16)
    for lo in range(0, d_ff, FF_CHUNK):
        gate = _dot(xn, win_ref[:, lo:lo + FF_CHUNK])
        up = _dot(xn, win_ref[:, d_ff + lo:d_ff + lo + FF_CHUNK])
        act_ref[:, lo:lo + FF_CHUNK] = (gate * jax.nn.sigmoid(gate) * up).astype(BF16)
    return x + 0.5 * _dot(act_ref[...], wout_ref[...])


def _ffn_kvc_kernel(x_ref, g1_ref, win_ref, wout_ref, gmix_ref, wkvc_ref, kg_ref, ones_ref,
                    h_ref, k_ref, v_ref, c_ref, act_ref):
    h = _swiglu_half_step(x_ref[...], g1_ref, win_ref, wout_ref, act_ref)
    h_ref[...] = h
    u = _rms(h, gmix_ref[...]).astype(BF16)
    kvc = _dot(u, wkvc_ref[...])
    k = kvc[:, :KV_DIM]
    ss = _group_sumsq(k, ones_ref[...])
    k_ref[...] = (k * lax.rsqrt(ss * (1.0 / HEAD_DIM) + 1e-6) * kg_ref[...]).astype(BF16)
    v_ref[...] = kvc[:, KV_DIM:2 * KV_DIM].T.astype(BF16)
    c_val = kvc[:, 2 * KV_DIM:2 * KV_DIM + CONV_DIM]
    c_gate = kvc[:, 2 * KV_DIM + CONV_DIM:]
    c_ref[...] = c_val * jax.nn.sigmoid(c_gate)


def _ffn_ple_kernel(x_ref, g1_ref, win_ref, wout_ref, gpe_ref, wpg_ref, p_ref, wpp_ref,
                    o_ref, act_ref):
    h = _swiglu_half_step(x_ref[...], g1_ref, win_ref, wout_ref, act_ref)
    t = _rms(h, gpe_ref[...]).astype(BF16)
    gate = jax.nn.sigmoid(_dot(t, wpg_ref[...]))
    pe = _dot(p_ref[...].astype(BF16), wpp_ref[...])
    o_ref[...] = h + pe * gate


def _mix_kernel(tiles_per_seq,
                h_ref, kp_ref, kc_ref, kn_ref, vp_ref, vc_ref, vn_ref, cp_ref, cc_ref, cn_ref,
                gmix_ref, wq_ref, wg_ref, qgain_ref, ones_ref, bias_ref, sink_ref,
                cw_ref, cb_ref, lng_ref, lnb_ref, wao_ref, wco_ref, wo_ref,
                o_ref, kext_ref, vext_ref, cext_ref, cshift_ref, y_ref, cact_ref, gate_ref):
    tm, d = h_ref.shape
    n_blocks = tm // BLOCK
    tile = pl.program_id(0) % tiles_per_seq
    is_first = tile == 0
    is_last = tile == tiles_per_seq - 1

    h = h_ref[...]
    u = _rms(h, gmix_ref[...]).astype(BF16)
    q = _dot(u, wq_ref[...])
    ss = _group_sumsq(q, ones_ref[...])
    qn = (q * lax.rsqrt(ss * (1.0 / HEAD_DIM) + 1e-6) * qgain_ref[...]).astype(BF16)

    cext_ref[:CONV_HALO] = jnp.where(is_first, 0.0, cp_ref[...])
    cext_ref[CONV_HALO:CONV_HALO + tm] = cc_ref[...]
    cext_ref[CONV_HALO + tm:] = jnp.where(is_last, 0.0, cn_ref[...])
    span = cshift_ref.shape[1]
    for s in range(1, SUBLANES):
        cshift_ref[s - 1] = cext_ref[s:s + span, :]

    n_chunks = tm // CONV_ROWS
    gate_cols = gate_ref.shape[1] // n_chunks
    for ci in range(n_chunks):
        r0 = ci * CONV_ROWS
        accs = []
        for l0 in range(0, CONV_DIM, LANES):
            acc = jnp.zeros((CONV_ROWS, LANES), F32)
            for w in range(CONV_WIDTH):
                off = CONV_HALO - CONV_PAD + w
                base = r0 + off - off % SUBLANES
                if off % SUBLANES == 0:
                    win = cext_ref[base:base + CONV_ROWS, l0:l0 + LANES]
                else:
                    win = cshift_ref[off % SUBLANES - 1, base:base + CONV_ROWS, l0:l0 + LANES]
                acc = acc + win * cw_ref[w:w + 1, l0:l0 + LANES]
            accs.append(acc)
        c = jnp.concatenate(accs, axis=1) + cb_ref[...]
        mu = jnp.mean(c, axis=-1, keepdims=True)
        var = jnp.mean(jnp.square(c - mu), axis=-1, keepdims=True)
        c = (c - mu) * lax.rsqrt(var + 1e-5) * lng_ref[...] + lnb_ref[...]
        cact_ref[r0:r0 + CONV_ROWS, :] = (c * jax.nn.sigmoid(c)).astype(BF16)
        g0 = ci * gate_cols
        gate_ref[:, g0:g0 + gate_cols] = jax.nn.sigmoid(_dot(u, wg_ref[:, g0:g0 + gate_cols]))
    y_conv = _dot(cact_ref[...], wco_ref[...])

    kext_ref[:BLOCK] = kp_ref[...]
    kext_ref[BLOCK:BLOCK + tm] = kc_ref[...]
    kext_ref[BLOCK + tm:] = kn_ref[...]
    vext_ref[:, :BLOCK] = vp_ref[...]
    vext_ref[:, BLOCK:BLOCK + tm] = vc_ref[...]
    vext_ref[:, BLOCK + tm:] = vn_ref[...]

    def scores(r, kv):
        if r == 0:
            table = jnp.where(is_first, 0, 1)
        elif r == n_blocks - 1:
            table = jnp.where(is_last, 2, 1)
        else:
            table = 1
        rows = slice(r * BLOCK, (r + 1) * BLOCK)
        q4 = jnp.concatenate(
            [qn[rows, (kv * GROUP + j) * HEAD_DIM:(kv * GROUP + j + 1) * HEAD_DIM]
             for j in range(GROUP)], axis=0)
        kw = kext_ref[r * BLOCK:(r + 3) * BLOCK, kv * HEAD_DIM:(kv + 1) * HEAD_DIM]
        return lax.dot_general(kw, q4, (((1,), (1,)), ((), ())),
                               preferred_element_type=F32) + bias_ref[table, kv]

    def softmax(s, kv):
        m = jnp.max(s, axis=0, keepdims=True)
        e = jnp.exp2(s - m)
        denom = jnp.sum(e, axis=0, keepdims=True) + jnp.exp2(sink_ref[kv] - m)
        return e.astype(BF16), 1.0 / denom

    def weighted_values(e, inv_denom, r, kv):
        vwt = vext_ref[kv * HEAD_DIM:(kv + 1) * HEAD_DIM, r * BLOCK:(r + 3) * BLOCK]
        o = _dot(vwt, e) * inv_denom
        for pair in range(GROUP // 2):
            two = jnp.concatenate([o[:, (2 * pair) * BLOCK:(2 * pair + 1) * BLOCK],
                                   o[:, (2 * pair + 1) * BLOCK:(2 * pair + 2) * BLOCK]], axis=0)
            col = (kv * GROUP + 2 * pair) * HEAD_DIM
            y_ref[r * BLOCK:(r + 1) * BLOCK, col:col + 2 * HEAD_DIM] = two.T.astype(BF16)

    units = [(r, kv) for r in range(n_blocks) for kv in range(KV_HEADS)]
    s_of, p_of = {}, {}
    for t in range(len(units) + 2):
        if t < len(units):
            s_of[t] = scores(*units[t])
        if 0 <= t - 1 < len(units):
            p_of[t - 1] = softmax(s_of.pop(t - 1), units[t - 1][1])
        if 0 <= t - 2 < len(units):
            weighted_values(*p_of.pop(t - 2), *units[t - 2])
    y_attn = _dot(y_ref[...], wao_ref[...])

    merged = gate_ref[:, :d] * y_attn + gate_ref[:, d:] * y_conv
    o_ref[...] = h + _dot(merged.astype(BF16), wo_ref[...])


def _bias_tables(rel_bias):
    half = NUM_BUCKETS // 2
    max_exact = half // 2
    period = 4 * BLOCK
    k = np.arange(period)
    rel = np.where(k < 3 * BLOCK, k - BLOCK, k - BLOCK - period)
    n = np.abs(rel)
    nf = np.maximum(n, 1).astype(np.float64)
    large = max_exact + (np.log(nf / max_exact) / math.log(MAX_DISTANCE / max_exact)
                         * (half - max_exact)).astype(np.int32)
    buckets = np.where(rel > 0, half, 0) + np.where(n < max_exact, n, np.minimum(large, half - 1))
    vec = jnp.where((n <= BLOCK)[None, :], rel_bias.astype(F32)[buckets].T, NEG_INF)
    skew = jnp.tile(vec, (1, BLOCK))[:, :BLOCK * (period - 1)].reshape(N_HEADS, BLOCK, period - 1)
    bias = skew[:, :, :3 * BLOCK].reshape(KV_HEADS, GROUP * BLOCK, 3 * BLOCK)
    col = np.arange(3 * BLOCK)
    edge = np.stack([col >= BLOCK, col >= 0, col < 2 * BLOCK])
    tables = jnp.where(edge[:, None, None, :], bias[None], NEG_INF)
    return jnp.swapaxes(tables, -1, -2) * LOG2_E


def _blockdiag_ones(n):
    idx = np.arange(n) // HEAD_DIM
    return jnp.asarray(idx[:, None] == idx[None, :], dtype=BF16)


def _whole(arr):
    zeros = (0,) * arr.ndim
    return pl.BlockSpec(arr.shape, lambda i: zeros, pipeline_mode=pl.Buffered(1))


def _layer(arr, layer):
    index = (layer,) + (0,) * (arr.ndim - 1)
    return pl.BlockSpec((None,) + arr.shape[1:], lambda i: index, pipeline_mode=pl.Buffered(1))


def _rows(width, tm=TILE_M):
    return pl.BlockSpec((tm, width), lambda i: (i, 0))


_PARAMS = pltpu.CompilerParams(dimension_semantics=("arbitrary",),
                               vmem_limit_bytes=VMEM_LIMIT_BYTES)


def _ffn_kvc(layer, x, g1, win, wout, gmix, wkvc, kg, ones):
    m, d = x.shape
    d_ff = wout.shape[1]
    params = (g1, win, wout, gmix, wkvc, kg)
    return pl.pallas_call(
        _ffn_kvc_kernel,
        grid=(m // TILE_M,),
        in_specs=[_rows(d)] + [_layer(a, layer) for a in params] + [_whole(ones)],
        out_specs=[_rows(d), _rows(KV_DIM), pl.BlockSpec((KV_DIM, TILE_M), lambda i: (0, i)),
                   _rows(CONV_DIM)],
        out_shape=[jax.ShapeDtypeStruct((m, d), F32), jax.ShapeDtypeStruct((m, KV_DIM), BF16),
                   jax.ShapeDtypeStruct((KV_DIM, m), BF16), jax.ShapeDtypeStruct((m, CONV_DIM), F32)],
        scratch_shapes=[pltpu.VMEM((TILE_M, d_ff), BF16)],
        compiler_params=_PARAMS,
        name="ffn_kvc",
    )(x, *params, ones)


def _ffn_ple(layer, x, g1, win, wout, gpe, wpg, p, wpp):
    m, d = x.shape
    d_ff = wout.shape[1]
    p_spec = pl.BlockSpec((None, TILE_M, p.shape[2]), lambda i: (layer, i, 0))
    return pl.pallas_call(
        _ffn_ple_kernel,
        grid=(m // TILE_M,),
        in_specs=[_rows(d)] + [_layer(a, layer) for a in (g1, win, wout, gpe, wpg)]
                 + [p_spec, _layer(wpp, layer)],
        out_specs=_rows(d),
        out_shape=jax.ShapeDtypeStruct((m, d), F32),
        scratch_shapes=[pltpu.VMEM((TILE_M, d_ff), BF16)],
        compiler_params=_PARAMS,
        name="ffn_ple",
    )(x, g1, win, wout, gpe, wpg, p, wpp)


def _mix(layer, h, k, v, c, seq, gmix, wq, wg, qgain, ones, bias, sink, cw, cb, lng, lnb,
         wao, wco, wo):
    m, d = h.shape
    tm = TILE_M
    tiles_per_seq = seq // tm
    kv_blocks = tm // BLOCK
    c_blocks = tm // CONV_HALO
    last_kv = m // BLOCK - 1
    last_c = m // CONV_HALO - 1

    def halo(rows, width, stride, last, side):
        if side < 0:
            return pl.BlockSpec((rows, width), lambda i: (jnp.maximum(i * stride - 1, 0), 0))
        return pl.BlockSpec((rows, width), lambda i: (jnp.minimum((i + 1) * stride, last), 0))

    k_specs = [halo(BLOCK, KV_DIM, kv_blocks, last_kv, -1), _rows(KV_DIM),
               halo(BLOCK, KV_DIM, kv_blocks, last_kv, +1)]
    v_specs = [pl.BlockSpec((KV_DIM, BLOCK), lambda i: (0, jnp.maximum(i * kv_blocks - 1, 0))),
               pl.BlockSpec((KV_DIM, tm), lambda i: (0, i)),
               pl.BlockSpec((KV_DIM, BLOCK), lambda i: (0, jnp.minimum((i + 1) * kv_blocks, last_kv)))]
    c_specs = [halo(CONV_HALO, CONV_DIM, c_blocks, last_c, -1), _rows(CONV_DIM),
               halo(CONV_HALO, CONV_DIM, c_blocks, last_c, +1)]
    shift_rows = tm + (CONV_HALO - CONV_PAD + CONV_WIDTH - 1) // SUBLANES * SUBLANES
    lay = lambda a: _layer(a, layer)
    return pl.pallas_call(
        functools.partial(_mix_kernel, tiles_per_seq),
        grid=(m // tm,),
        in_specs=[_rows(d)] + k_specs + v_specs + c_specs + [
            lay(gmix), lay(wq), lay(wg), lay(qgain), _whole(ones), _whole(bias), lay(sink),
            lay(cw), lay(cb), lay(lng), lay(lnb), lay(wao), lay(wco), lay(wo)],
        out_specs=_rows(d),
        out_shape=jax.ShapeDtypeStruct((m, d), F32),
        scratch_shapes=[pltpu.VMEM((tm + 2 * BLOCK, KV_DIM), BF16),
                        pltpu.VMEM((KV_DIM, tm + 2 * BLOCK), BF16),
                        pltpu.VMEM((tm + 2 * CONV_HALO, CONV_DIM), F32),
                        pltpu.VMEM((SUBLANES - 1, shift_rows, CONV_DIM), F32),
                        pltpu.VMEM((tm, Q_DIM), BF16),
                        pltpu.VMEM((tm, CONV_DIM), BF16),
                        pltpu.VMEM((tm, 2 * d), F32)],
        compiler_params=_PARAMS,
        name="mix",
    )(h, k, k, k, v, v, v, c, c, c, gmix, wq, wg, qgain, ones, bias, sink, cw, cb, lng, lnb,
      wao, wco, wo)


def kernel(x, p, rel_bias, norm_ffn1, w_ffn1_in, w_ffn1_out, norm_mix, w_in, q_norm, k_norm, sink, conv_w, conv_b, conv_ln_g, conv_ln_b, w_attn_out, w_conv_out, w_o, norm_ffn2, w_ffn2_in, w_ffn2_out, norm_pe, w_pe_gate, w_pe_proj):
    batch, seq, d = x.shape
    depth = p.shape[0]
    m = batch * seq
    assert seq % TILE_M == 0 and TILE_M % BLOCK == 0 and seq // BLOCK >= 2

    kvc_lo, kvc_hi = Q_DIM, Q_DIM + 2 * KV_DIM + 2 * CONV_DIM
    w_q = w_in[:, :, :kvc_lo].astype(BF16)
    w_kvc = w_in[:, :, kvc_lo:kvc_hi].astype(BF16)
    w_g = w_in[:, :, kvc_hi:].astype(BF16)
    w1_in, w1_out = w_ffn1_in.astype(BF16), w_ffn1_out.astype(BF16)
    w2_in, w2_out = w_ffn2_in.astype(BF16), w_ffn2_out.astype(BF16)
    w_ao, w_co, w_oo = w_attn_out.astype(BF16), w_conv_out.astype(BF16), w_o.astype(BF16)
    w_pg, w_pp = w_pe_gate.astype(BF16), w_pe_proj.astype(BF16)

    bias = _bias_tables(rel_bias)
    ones_q, ones_k = _blockdiag_ones(Q_DIM), _blockdiag_ones(KV_DIM)
    vec = lambda a: a[:, None, :]
    q_gain = vec(jnp.tile(q_norm, (1, N_HEADS)) * (HEAD_DIM ** -0.5 * LOG2_E))
    k_gain = vec(jnp.tile(k_norm, (1, KV_HEADS)))
    sink_rows = jnp.repeat(sink.reshape(depth, KV_HEADS, 1, GROUP), BLOCK, axis=-1) * LOG2_E
    g_ffn1, g_mix, g_ffn2, g_pe = vec(norm_ffn1), vec(norm_mix), vec(norm_ffn2), vec(norm_pe)
    cb, lng, lnb = vec(conv_b), vec(conv_ln_g), vec(conv_ln_b)

    xf = x.reshape(m, d)
    pf = p.reshape(depth, m, p.shape[-1])
    for i in range(depth):
        h, k, v, c = _ffn_kvc(i, xf, g_ffn1, w1_in, w1_out, g_mix, w_kvc, k_gain, ones_k)
        h = _mix(i, h, k, v, c, seq, g_mix, w_q, w_g, q_gain, ones_q, bias, sink_rows,
                 conv_w, cb, lng, lnb, w_ao, w_co, w_oo)
        xf = _ffn_ple(i, h, g_ffn2, w2_in, w2_out, g_pe, w_pg, pf, w_pp)
    return xf.reshape(batch, seq, d)
```

```python
import functools
import math

import numpy as np
import jax
import jax.numpy as jnp
from jax import lax
from jax.experimental import pallas as pl
from jax.experimental.pallas import tpu as pltpu

N_HEADS = 8
KV_HEADS = 2
GROUP = N_HEADS // KV_HEADS
HEAD_DIM = 64
Q_DIM = N_HEADS * HEAD_DIM
KV_DIM = KV_HEADS * HEAD_DIM
BLOCK = 128
NUM_BUCKETS = 32
MAX_DISTANCE = 128
CONV_DIM = 512
CONV_WIDTH = 31
CONV_PAD = CONV_WIDTH // 2
CONV_HALO = 16
NEG_INF = -1e9
LOG2_E = math.log2(math.e)

TILE_M = 512
FF_CHUNK = 256
CONV_ROWS = 64
CONV_TILES = 13
LANES = 128
SUBLANES = 8
VMEM_LIMIT_BYTES = 56 * 1024 * 1024

F32 = jnp.float32
BF16 = jnp.bfloat16


def _rms(x, g, eps=1e-6):
    return x * lax.rsqrt(jnp.mean(x * x, axis=-1, keepdims=True) + eps) * g


def _dot(a, b):
    return jnp.dot(a, b, preferred_element_type=F32)


def _group_sumsq(x, ones_blockdiag):
    x2 = x * x
    hi = x2.astype(BF16)
    lo = (x2 - hi.astype(F32)).astype(BF16)
    return _dot(hi, ones_blockdiag) + _dot(lo, ones_blockdiag)


def _swiglu_half_step(x, g_ref, win_ref, wout_ref, act_ref):
    d_ff = wout_ref.shape[0]
    xn = _rms(x, g_ref[...]).astype(BF16)
    for lo in range(0, d_ff, FF_CHUNK):
        gate = _dot(xn, win_ref[:, lo:lo + FF_CHUNK])
        up = _dot(xn, win_ref[:, d_ff + lo:d_ff + lo + FF_CHUNK])
        act_ref[:, lo:lo + FF_CHUNK] = (gate * jax.nn.sigmoid(gate) * up).astype(BF16)
    return x + 0.5 * _dot(act_ref[...], wout_ref[...])


def _ffn_kvc_kernel(x_ref, g1_ref, win_ref, wout_ref, gmix_ref, wkvc_ref, kg_ref, ones_ref,
                    h_ref, k_ref, v_ref, c_ref, act_ref):
    h = _swiglu_half_step(x_ref[...], g1_ref, win_ref, wout_ref, act_ref)
    h_ref[...] = h
    u = _rms(h, gmix_ref[...]).astype(BF16)
    kvc = _dot(u, wkvc_ref[...])
    k = kvc[:, :KV_DIM]
    ss = _group_sumsq(k, ones_ref[...])
    k_ref[...] = (k * lax.rsqrt(ss * (1.0 / HEAD_DIM) + 1e-6) * kg_ref[...]).astype(BF16)
    v_ref[...] = kvc[:, KV_DIM:2 * KV_DIM].T.astype(BF16)
    c_val = kvc[:, 2 * KV_DIM:2 * KV_DIM + CONV_DIM]
    c_gate = kvc[:, 2 * KV_DIM + CONV_DIM:]
    c_ref[...] = c_val * jax.nn.sigmoid(c_gate)


def _ffn_ple_kernel(x_ref, g1_ref, win_ref, wout_ref, gpe_ref, wpg_ref, p_ref, wpp_ref,
                    o_ref, act_ref):
    h = _swiglu_half_step(x_ref[...], g1_ref, win_ref, wout_ref, act_ref)
    t = _rms(h, gpe_ref[...]).astype(BF16)
    gate = jax.nn.sigmoid(_dot(t, wpg_ref[...]))
    pe = _dot(p_ref[...].astype(BF16), wpp_ref[...])
    o_ref[...] = h + pe * gate


def _mix_kernel(tiles_per_seq,
                h_ref, kp_ref, kc_ref, kn_ref, vp_ref, vc_ref, vn_ref, cp_ref, cc_ref, cn_ref,
                gmix_ref, wq_ref, wg_ref, qgain_ref, ones_ref, bias_ref, sink_ref,
                cw_ref, cb_ref, lng_ref, lnb_ref, wao_ref, wco_ref, wo_ref,
                o_ref, kext_ref, vext_ref, cext_ref, conv_ref, y_ref, cact_ref, gate_ref):
    tm, d = h_ref.shape
    n_blocks = tm // BLOCK
    tile = pl.program_id(0) % tiles_per_seq
    is_first = tile == 0
    is_last = tile == tiles_per_seq - 1

    h = h_ref[...]
    u = _rms(h, gmix_ref[...]).astype(BF16)
    q = _dot(u, wq_ref[...])
    ss = _group_sumsq(q, ones_ref[...])
    qn = (q * lax.rsqrt(ss * (1.0 / HEAD_DIM) + 1e-6) * qgain_ref[...]).astype(BF16)

    lane_groups = CONV_DIM // LANES
    seg = conv_ref.shape[1] // SUBLANES
    assert seg % 2 == 1 and seg * SUBLANES >= tm and seg % CONV_TILES == 0
    for g in range(lane_groups):
        lanes = slice(g * LANES, (g + 1) * LANES)
        cext_ref[g, :CONV_HALO] = jnp.where(is_first, 0.0, cp_ref[:, lanes])
        cext_ref[g, CONV_HALO:CONV_HALO + tm] = cc_ref[:, lanes]
        cext_ref[g, CONV_HALO + tm:] = jnp.where(is_last, 0.0, cn_ref[:, lanes])
    first_tap = CONV_HALO - CONV_PAD
    def conv_block(g, j0):
        lanes = slice(g * LANES, (g + 1) * LANES)
        windows = {}

        def window(j):
            if j not in windows:
                windows[j] = cext_ref[g, pl.ds(first_tap + j, SUBLANES, stride=seg), :]
            return windows[j]

        accs = [jnp.zeros((SUBLANES, LANES), F32)] * CONV_TILES
        for w in range(CONV_WIDTH):
            tap = cw_ref[w:w + 1, lanes]
            accs = [acc + window(j0 + t + w) * tap for t, acc in enumerate(accs)]
        for t, acc in enumerate(accs):
            conv_ref[g, pl.ds(j0 + t, SUBLANES, stride=seg), :] = acc + cb_ref[:, lanes]

    for g in range(lane_groups):
        for j0 in range(0, seg, CONV_TILES):
            conv_block(g, j0)
    for r0 in range(0, tm, CONV_ROWS):
        c = jnp.concatenate([conv_ref[g, r0:r0 + CONV_ROWS, :] for g in range(lane_groups)], axis=1)
        mu = jnp.mean(c, axis=-1, keepdims=True)
        var = jnp.mean(jnp.square(c - mu), axis=-1, keepdims=True)
        c = (c - mu) * lax.rsqrt(var + 1e-5) * lng_ref[...] + lnb_ref[...]
        cact_ref[r0:r0 + CONV_ROWS, :] = (c * jax.nn.sigmoid(c)).astype(BF16)
    y_conv = _dot(cact_ref[...], wco_ref[...])

    n_chunks = tm // CONV_ROWS
    gate_cols = gate_ref.shape[1] // n_chunks

    def gate_chunk(ci):
        g0 = ci * gate_cols
        gate_ref[:, g0:g0 + gate_cols] = jax.nn.sigmoid(_dot(u, wg_ref[:, g0:g0 + gate_cols]))

    kext_ref[:BLOCK] = kp_ref[...]
    kext_ref[BLOCK:BLOCK + tm] = kc_ref[...]
    kext_ref[BLOCK + tm:] = kn_ref[...]
    vext_ref[:, :BLOCK] = vp_ref[...]
    vext_ref[:, BLOCK:BLOCK + tm] = vc_ref[...]
    vext_ref[:, BLOCK + tm:] = vn_ref[...]

    def scores(r, kv):
        if r == 0:
            table = jnp.where(is_first, 0, 1)
        elif r == n_blocks - 1:
            table = jnp.where(is_last, 2, 1)
        else:
            table = 1
        rows = slice(r * BLOCK, (r + 1) * BLOCK)
        q4 = jnp.concatenate(
            [qn[rows, (kv * GROUP + j) * HEAD_DIM:(kv * GROUP + j + 1) * HEAD_DIM]
             for j in range(GROUP)], axis=0)
        kw = kext_ref[r * BLOCK:(r + 3) * BLOCK, kv * HEAD_DIM:(kv + 1) * HEAD_DIM]
        return lax.dot_general(kw, q4, (((1,), (1,)), ((), ())),
                               preferred_element_type=F32) + bias_ref[table, kv]

    def softmax(s, kv):
        m = jnp.max(s, axis=0, keepdims=True)
        e = jnp.exp2(s - m)
        denom = jnp.sum(e, axis=0, keepdims=True) + jnp.exp2(sink_ref[kv] - m)
        return e.astype(BF16), 1.0 / denom

    def weighted_values(e, inv_denom, r, kv):
        vwt = vext_ref[kv * HEAD_DIM:(kv + 1) * HEAD_DIM, r * BLOCK:(r + 3) * BLOCK]
        o = _dot(vwt, e) * inv_denom
        for pair in range(GROUP // 2):
            two = jnp.concatenate([o[:, (2 * pair) * BLOCK:(2 * pair + 1) * BLOCK],
                                   o[:, (2 * pair + 1) * BLOCK:(2 * pair + 2) * BLOCK]], axis=0)
            col = (kv * GROUP + 2 * pair) * HEAD_DIM
            y_ref[r * BLOCK:(r + 1) * BLOCK, col:col + 2 * HEAD_DIM] = two.T.astype(BF16)

    units = [(r, kv) for r in range(n_blocks) for kv in range(KV_HEADS)]
    s_of, p_of = {}, {}
    for t in range(len(units) + 2):
        if t < len(units):
            s_of[t] = scores(*units[t])
        if t < n_chunks:
            gate_chunk(t)
        if 0 <= t - 1 < len(units):
            p_of[t - 1] = softmax(s_of.pop(t - 1), units[t - 1][1])
        if 0 <= t - 2 < len(units):
            weighted_values(*p_of.pop(t - 2), *units[t - 2])
    y_attn = _dot(y_ref[...], wao_ref[...])

    merged = gate_ref[:, :d] * y_attn + gate_ref[:, d:] * y_conv
    o_ref[...] = h + _dot(merged.astype(BF16), wo_ref[...])


def _bias_tables(rel_bias):
    half = NUM_BUCKETS // 2
    max_exact = half // 2
    period = 4 * BLOCK
    k = np.arange(period)
    rel = np.where(k < 3 * BLOCK, k - BLOCK, k - BLOCK - period)
    n = np.abs(rel)
    nf = np.maximum(n, 1).astype(np.float64)
    large = max_exact + (np.log(nf / max_exact) / math.log(MAX_DISTANCE / max_exact)
                         * (half - max_exact)).astype(np.int32)
    buckets = np.where(rel > 0, half, 0) + np.where(n < max_exact, n, np.minimum(large, half - 1))
    vec = jnp.where((n <= BLOCK)[None, :], rel_bias.astype(F32)[buckets].T, NEG_INF)
    skew = jnp.tile(vec, (1, BLOCK))[:, :BLOCK * (period - 1)].reshape(N_HEADS, BLOCK, period - 1)
    bias = skew[:, :, :3 * BLOCK].reshape(KV_HEADS, GROUP * BLOCK, 3 * BLOCK)
    col = np.arange(3 * BLOCK)
    edge = np.stack([col >= BLOCK, col >= 0, col < 2 * BLOCK])
    tables = jnp.where(edge[:, None, None, :], bias[None], NEG_INF)
    return jnp.swapaxes(tables, -1, -2) * LOG2_E


def _blockdiag_ones(n):
    idx = np.arange(n) // HEAD_DIM
    return jnp.asarray(idx[:, None] == idx[None, :], dtype=BF16)


def _whole(arr):
    zeros = (0,) * arr.ndim
    return pl.BlockSpec(arr.shape, lambda i: zeros, pipeline_mode=pl.Buffered(1))


def _layer(arr, layer):
    index = (layer,) + (0,) * (arr.ndim - 1)
    return pl.BlockSpec((None,) + arr.shape[1:], lambda i: index, pipeline_mode=pl.Buffered(1))


def _rows(width, tm=TILE_M):
    return pl.BlockSpec((tm, width), lambda i: (i, 0))


_PARAMS = pltpu.CompilerParams(dimension_semantics=("arbitrary",),
                               vmem_limit_bytes=VMEM_LIMIT_BYTES)


def _ffn_kvc(layer, x, g1, win, wout, gmix, wkvc, kg, ones):
    m, d = x.shape
    d_ff = wout.shape[1]
    params = (g1, win, wout, gmix, wkvc, kg)
    return pl.pallas_call(
        _ffn_kvc_kernel,
        grid=(m // TILE_M,),
        in_specs=[_rows(d)] + [_layer(a, layer) for a in params] + [_whole(ones)],
        out_specs=[_rows(d), _rows(KV_DIM), pl.BlockSpec((KV_DIM, TILE_M), lambda i: (0, i)),
                   _rows(CONV_DIM)],
        out_shape=[jax.ShapeDtypeStruct((m, d), F32), jax.ShapeDtypeStruct((m, KV_DIM), BF16),
                   jax.ShapeDtypeStruct((KV_DIM, m), BF16), jax.ShapeDtypeStruct((m, CONV_DIM), F32)],
        scratch_shapes=[pltpu.VMEM((TILE_M, d_ff), BF16)],
        compiler_params=_PARAMS,
        name="ffn_kvc",
    )(x, *params, ones)


def _ffn_ple(layer, x, g1, win, wout, gpe, wpg, p, wpp):
    m, d = x.shape
    d_ff = wout.shape[1]
    p_spec = pl.BlockSpec((None, TILE_M, p.shape[2]), lambda i: (layer, i, 0))
    return pl.pallas_call(
        _ffn_ple_kernel,
        grid=(m // TILE_M,),
        in_specs=[_rows(d)] + [_layer(a, layer) for a in (g1, win, wout, gpe, wpg)]
                 + [p_spec, _layer(wpp, layer)],
        out_specs=_rows(d),
        out_shape=jax.ShapeDtypeStruct((m, d), F32),
        scratch_shapes=[pltpu.VMEM((TILE_M, d_ff), BF16)],
        compiler_params=_PARAMS,
        name="ffn_ple",
    )(x, g1, win, wout, gpe, wpg, p, wpp)


def _mix(layer, h, k, v, c, seq, gmix, wq, wg, qgain, ones, bias, sink, cw, cb, lng, lnb,
         wao, wco, wo):
    m, d = h.shape
    tm = TILE_M
    tiles_per_seq = seq // tm
    kv_blocks = tm // BLOCK
    last_kv = m // BLOCK - 1
    seg = tm // SUBLANES + 1
    reach = SUBLANES * seg - tm + CONV_PAD
    next_halo = next(n for n in (16, 32, 64, 128) if n >= reach)
    assert tm % next_halo == 0 and tm % CONV_HALO == 0

    def halo(rows, width, stride, last, side):
        if side < 0:
            return pl.BlockSpec((rows, width), lambda i: (jnp.maximum(i * stride - 1, 0), 0))
        return pl.BlockSpec((rows, width), lambda i: (jnp.minimum((i + 1) * stride, last), 0))

    k_specs = [halo(BLOCK, KV_DIM, kv_blocks, last_kv, -1), _rows(KV_DIM),
               halo(BLOCK, KV_DIM, kv_blocks, last_kv, +1)]
    v_specs = [pl.BlockSpec((KV_DIM, BLOCK), lambda i: (0, jnp.maximum(i * kv_blocks - 1, 0))),
               pl.BlockSpec((KV_DIM, tm), lambda i: (0, i)),
               pl.BlockSpec((KV_DIM, BLOCK), lambda i: (0, jnp.minimum((i + 1) * kv_blocks, last_kv)))]
    c_specs = [halo(CONV_HALO, CONV_DIM, tm // CONV_HALO, m // CONV_HALO - 1, -1), _rows(CONV_DIM),
               halo(next_halo, CONV_DIM, tm // next_halo, m // next_halo - 1, +1)]
    lay = lambda a: _layer(a, layer)
    return pl.pallas_call(
        functools.partial(_mix_kernel, tiles_per_seq),
        grid=(m // tm,),
        in_specs=[_rows(d)] + k_specs + v_specs + c_specs + [
            lay(gmix), lay(wq), lay(wg), lay(qgain), _whole(ones), _whole(bias), lay(sink),
            lay(cw), lay(cb), lay(lng), lay(lnb), lay(wao), lay(wco), lay(wo)],
        out_specs=_rows(d),
        out_shape=jax.ShapeDtypeStruct((m, d), F32),
        scratch_shapes=[pltpu.VMEM((tm + 2 * BLOCK, KV_DIM), BF16),
                        pltpu.VMEM((KV_DIM, tm + 2 * BLOCK), BF16),
                        pltpu.VMEM((CONV_DIM // LANES, CONV_HALO + tm + next_halo, LANES), F32),
                        pltpu.VMEM((CONV_DIM // LANES, SUBLANES * seg, LANES), F32),
                        pltpu.VMEM((tm, Q_DIM), BF16),
                        pltpu.VMEM((tm, CONV_DIM), BF16),
                        pltpu.VMEM((tm, 2 * d), F32)],
        compiler_params=_PARAMS,
        name="mix",
    )(h, k, k, k, v, v, v, c, c, c, gmix, wq, wg, qgain, ones, bias, sink, cw, cb, lng, lnb,
      wao, wco, wo)


def kernel(x, p, rel_bias, norm_ffn1, w_ffn1_in, w_ffn1_out, norm_mix, w_in, q_norm, k_norm, sink, conv_w, conv_b, conv_ln_g, conv_ln_b, w_attn_out, w_conv_out, w_o, norm_ffn2, w_ffn2_in, w_ffn2_out, norm_pe, w_pe_gate, w_pe_proj):
    batch, seq, d = x.shape
    depth = p.shape[0]
    m = batch * seq
    assert seq % TILE_M == 0 and TILE_M % BLOCK == 0 and seq // BLOCK >= 2

    kvc_lo, kvc_hi = Q_DIM, Q_DIM + 2 * KV_DIM + 2 * CONV_DIM
    w_q = w_in[:, :, :kvc_lo].astype(BF16)
    w_kvc = w_in[:, :, kvc_lo:kvc_hi].astype(BF16)
    w_g = w_in[:, :, kvc_hi:].astype(BF16)
    w1_in, w1_out = w_ffn1_in.astype(BF16), w_ffn1_out.astype(BF16)
    w2_in, w2_out = w_ffn2_in.astype(BF16), w_ffn2_out.astype(BF16)
    w_ao, w_co, w_oo = w_attn_out.astype(BF16), w_conv_out.astype(BF16), w_o.astype(BF16)
    w_pg, w_pp = w_pe_gate.astype(BF16), w_pe_proj.astype(BF16)

    bias = _bias_tables(rel_bias)
    ones_q, ones_k = _blockdiag_ones(Q_DIM), _blockdiag_ones(KV_DIM)
    vec = lambda a: a[:, None, :]
    q_gain = vec(jnp.tile(q_norm, (1, N_HEADS)) * (HEAD_DIM ** -0.5 * LOG2_E))
    k_gain = vec(jnp.tile(k_norm, (1, KV_HEADS)))
    sink_rows = jnp.repeat(sink.reshape(depth, KV_HEADS, 1, GROUP), BLOCK, axis=-1) * LOG2_E
    g_ffn1, g_mix, g_ffn2, g_pe = vec(norm_ffn1), vec(norm_mix), vec(norm_ffn2), vec(norm_pe)
    cb, lng, lnb = vec(conv_b), vec(conv_ln_g), vec(conv_ln_b)

    xf = x.reshape(m, d)
    pf = p.reshape(depth, m, p.shape[-1])
    for i in range(depth):
        h, k, v, c = _ffn_kvc(i, xf, g_ffn1, w1_in, w1_out, g_mix, w_kvc, k_gain, ones_k)
        h = _mix(i, h, k, v, c, seq, g_mix, w_q, w_g, q_gain, ones_q, bias, sink_rows,
                 conv_w, cb, lng, lnb, w_ao, w_co, w_oo)
        xf = _ffn_ple(i, h, g_ffn2, w2_in, w2_out, g_pe, w_pg, pf, w_pp)
    return xf.reshape(batch, seq, d)
```

```python
import functools
import math

import numpy as np
import jax
import jax.numpy as jnp
from jax import lax
from jax.experimental import pallas as pl
from jax.experimental.pallas import tpu as pltpu

N_HEADS = 8
KV_HEADS = 2
GROUP = N_HEADS // KV_HEADS
HEAD_DIM = 64
Q_DIM = N_HEADS * HEAD_DIM
KV_DIM = KV_HEADS * HEAD_DIM
BLOCK = 128
NUM_BUCKETS = 32
MAX_DISTANCE = 128
CONV_DIM = 512
CONV_WIDTH = 31
CONV_PAD = CONV_WIDTH // 2
CONV_HALO = 16
NEG_INF = -1e9
LOG2_E = math.log2(math.e)

TILE_M = 512
TILE_M_PLE = 1024
FF_CHUNK = 256
FFN_ROWS = 256
CONV_ROWS = 64
CONV_TILES = 13
LANES = 128
SUBLANES = 8
VMEM_LIMIT_BYTES = 56 * 1024 * 1024

F32 = jnp.float32
BF16 = jnp.bfloat16


def _rms(x, g, eps=1e-6):
    return x * lax.rsqrt(jnp.mean(x * x, axis=-1, keepdims=True) + eps) * g


def _dot(a, b):
    return jnp.dot(a, b, preferred_element_type=F32)


def _group_sumsq(x, ones_blockdiag):
    x2 = x * x
    hi = x2.astype(BF16)
    lo = (x2 - hi.astype(F32)).astype(BF16)
    return _dot(hi, ones_blockdiag) + _dot(lo, ones_blockdiag)


def _swiglu_half_step(x_ref, g_ref, win_ref, wout_ref, act_ref):
    d_ff = wout_ref.shape[0]
    groups = [slice(r0, r0 + FFN_ROWS) for r0 in range(0, x_ref.shape[0], FFN_ROWS)]
    for rows in groups:
        xn = _rms(x_ref[rows, :], g_ref[...]).astype(BF16)
        for lo in range(0, d_ff, FF_CHUNK):
            gate = _dot(xn, win_ref[:, lo:lo + FF_CHUNK])
            up = _dot(xn, win_ref[:, d_ff + lo:d_ff + lo + FF_CHUNK])
            act_ref[rows, lo:lo + FF_CHUNK] = (gate * jax.nn.sigmoid(gate) * up).astype(BF16)
    return [(rows, x_ref[rows, :] + 0.5 * _dot(act_ref[rows, :], wout_ref[...])) for rows in groups]


def _ffn_kvc_kernel(x_ref, g1_ref, win_ref, wout_ref, gmix_ref, wkvc_ref, kg_ref, ones_ref,
                    h_ref, k_ref, v_ref, c_ref, act_ref):
    for rows, h in _swiglu_half_step(x_ref, g1_ref, win_ref, wout_ref, act_ref):
        h_ref[rows, :] = h
        u = _rms(h, gmix_ref[...]).astype(BF16)
        kvc = _dot(u, wkvc_ref[...])
        k = kvc[:, :KV_DIM]
        ss = _group_sumsq(k, ones_ref[...])
        k_ref[rows, :] = (k * lax.rsqrt(ss * (1.0 / HEAD_DIM) + 1e-6) * kg_ref[...]).astype(BF16)
        v_ref[:, rows] = kvc[:, KV_DIM:2 * KV_DIM].T.astype(BF16)
        c_val = kvc[:, 2 * KV_DIM:2 * KV_DIM + CONV_DIM]
        c_gate = kvc[:, 2 * KV_DIM + CONV_DIM:]
        c_ref[rows, :] = c_val * jax.nn.sigmoid(c_gate)


def _ffn_ple_kernel(x_ref, g1_ref, win_ref, wout_ref, gpe_ref, wpg_ref, p_ref, wpp_ref,
                    o_ref, act_ref):
    for rows, h in _swiglu_half_step(x_ref, g1_ref, win_ref, wout_ref, act_ref):
        t = _rms(h, gpe_ref[...]).astype(BF16)
        gate = jax.nn.sigmoid(_dot(t, wpg_ref[...]))
        pe = _dot(p_ref[rows, :].astype(BF16), wpp_ref[...])
        o_ref[rows, :] = h + pe * gate


def _mix_kernel(tiles_per_seq,
                h_ref, kp_ref, kc_ref, kn_ref, vp_ref, vc_ref, vn_ref, cp_ref, cc_ref, cn_ref,
                gmix_ref, wq_ref, wg_ref, qgain_ref, ones_ref, bias_ref, sink_ref,
                cw_ref, cb_ref, lng_ref, lnb_ref, wao_ref, wco_ref, wo_ref,
                o_ref, kext_ref, vext_ref, cext_ref, conv_ref, y_ref, cact_ref, gate_ref):
    tm, d = h_ref.shape
    n_blocks = tm // BLOCK
    tile = pl.program_id(0) % tiles_per_seq
    is_first = tile == 0
    is_last = tile == tiles_per_seq - 1

    h = h_ref[...]
    u = _rms(h, gmix_ref[...]).astype(BF16)
    q = _dot(u, wq_ref[...])
    ss = _group_sumsq(q, ones_ref[...])
    qn = (q * lax.rsqrt(ss * (1.0 / HEAD_DIM) + 1e-6) * qgain_ref[...]).astype(BF16)

    lane_groups = CONV_DIM // LANES
    seg = conv_ref.shape[1] // SUBLANES
    assert seg % 2 == 1 and seg * SUBLANES >= tm and seg % CONV_TILES == 0
    for g in range(lane_groups):
        lanes = slice(g * LANES, (g + 1) * LANES)
        cext_ref[g, :CONV_HALO] = jnp.where(is_first, 0.0, cp_ref[:, lanes])
        cext_ref[g, CONV_HALO:CONV_HALO + tm] = cc_ref[:, lanes]
        cext_ref[g, CONV_HALO + tm:] = jnp.where(is_last, 0.0, cn_ref[:, lanes])
    first_tap = CONV_HALO - CONV_PAD
    def conv_block(g, j0):
        lanes = slice(g * LANES, (g + 1) * LANES)
        windows = {}

        def window(j):
            if j not in windows:
                windows[j] = cext_ref[g, pl.ds(first_tap + j, SUBLANES, stride=seg), :]
            return windows[j]

        accs = [jnp.zeros((SUBLANES, LANES), F32)] * CONV_TILES
        for w in range(CONV_WIDTH):
            tap = cw_ref[w:w + 1, lanes]
            accs = [acc + window(j0 + t + w) * tap for t, acc in enumerate(accs)]
        for t, acc in enumerate(accs):
            conv_ref[g, pl.ds(j0 + t, SUBLANES, stride=seg), :] = acc + cb_ref[:, lanes]

    for g in range(lane_groups):
        for j0 in range(0, seg, CONV_TILES):
            conv_block(g, j0)
    for r0 in range(0, tm, CONV_ROWS):
        c = jnp.concatenate([conv_ref[g, r0:r0 + CONV_ROWS, :] for g in range(lane_groups)], axis=1)
        mu = jnp.mean(c, axis=-1, keepdims=True)
        var = jnp.mean(jnp.square(c - mu), axis=-1, keepdims=True)
        c = (c - mu) * lax.rsqrt(var + 1e-5) * lng_ref[...] + lnb_ref[...]
        cact_ref[r0:r0 + CONV_ROWS, :] = (c * jax.nn.sigmoid(c)).astype(BF16)
    y_conv = _dot(cact_ref[...], wco_ref[...])

    n_chunks = tm // CONV_ROWS
    gate_cols = gate_ref.shape[1] // n_chunks

    def gate_chunk(ci):
        g0 = ci * gate_cols
        gate_ref[:, g0:g0 + gate_cols] = jax.nn.sigmoid(_dot(u, wg_ref[:, g0:g0 + gate_cols]))

    kext_ref[:BLOCK] = kp_ref[...]
    kext_ref[BLOCK:BLOCK + tm] = kc_ref[...]
    kext_ref[BLOCK + tm:] = kn_ref[...]
    vext_ref[:, :BLOCK] = vp_ref[...]
    vext_ref[:, BLOCK:BLOCK + tm] = vc_ref[...]
    vext_ref[:, BLOCK + tm:] = vn_ref[...]

    def scores(r, kv):
        if r == 0:
            table = jnp.where(is_first, 0, 1)
        elif r == n_blocks - 1:
            table = jnp.where(is_last, 2, 1)
        else:
            table = 1
        rows = slice(r * BLOCK, (r + 1) * BLOCK)
        q4 = jnp.concatenate(
            [qn[rows, (kv * GROUP + j) * HEAD_DIM:(kv * GROUP + j + 1) * HEAD_DIM]
             for j in range(GROUP)], axis=0)
        kw = kext_ref[r * BLOCK:(r + 3) * BLOCK, kv * HEAD_DIM:(kv + 1) * HEAD_DIM]
        return lax.dot_general(kw, q4, (((1,), (1,)), ((), ())),
                               preferred_element_type=F32) + bias_ref[table, kv]

    def softmax(s, kv):
        m = jnp.max(s, axis=0, keepdims=True)
        e = jnp.exp2(s - m)
        denom = jnp.sum(e, axis=0, keepdims=True) + jnp.exp2(sink_ref[kv] - m)
        return e.astype(BF16), 1.0 / denom

    def weighted_values(e, inv_denom, r, kv):
        vwt = vext_ref[kv * HEAD_DIM:(kv + 1) * HEAD_DIM, r * BLOCK:(r + 3) * BLOCK]
        o = _dot(vwt, e) * inv_denom
        for pair in range(GROUP // 2):
            two = jnp.concatenate([o[:, (2 * pair) * BLOCK:(2 * pair + 1) * BLOCK],
                                   o[:, (2 * pair + 1) * BLOCK:(2 * pair + 2) * BLOCK]], axis=0)
            col = (kv * GROUP + 2 * pair) * HEAD_DIM
            y_ref[r * BLOCK:(r + 1) * BLOCK, col:col + 2 * HEAD_DIM] = two.T.astype(BF16)

    units = [(r, kv) for r in range(n_blocks) for kv in range(KV_HEADS)]
    s_of, p_of = {}, {}
    for t in range(len(units) + 2):
        if t < len(units):
            s_of[t] = scores(*units[t])
        if t < n_chunks:
            gate_chunk(t)
        if 0 <= t - 1 < len(units):
            p_of[t - 1] = softmax(s_of.pop(t - 1), units[t - 1][1])
        if 0 <= t - 2 < len(units):
            weighted_values(*p_of.pop(t - 2), *units[t - 2])
    y_attn = _dot(y_ref[...], wao_ref[...])

    merged = gate_ref[:, :d] * y_attn + gate_ref[:, d:] * y_conv
    o_ref[...] = h + _dot(merged.astype(BF16), wo_ref[...])


def _bias_tables(rel_bias):
    half = NUM_BUCKETS // 2
    max_exact = half // 2
    period = 4 * BLOCK
    k = np.arange(period)
    rel = np.where(k < 3 * BLOCK, k - BLOCK, k - BLOCK - period)
    n = np.abs(rel)
    nf = np.maximum(n, 1).astype(np.float64)
    large = max_exact + (np.log(nf / max_exact) / math.log(MAX_DISTANCE / max_exact)
                         * (half - max_exact)).astype(np.int32)
    buckets = np.where(rel > 0, half, 0) + np.where(n < max_exact, n, np.minimum(large, half - 1))
    vec = jnp.where((n <= BLOCK)[None, :], rel_bias.astype(F32)[buckets].T, NEG_INF)
    skew = jnp.tile(vec, (1, BLOCK))[:, :BLOCK * (period - 1)].reshape(N_HEADS, BLOCK, period - 1)
    bias = skew[:, :, :3 * BLOCK].reshape(KV_HEADS, GROUP * BLOCK, 3 * BLOCK)
    col = np.arange(3 * BLOCK)
    edge = np.stack([col >= BLOCK, col >= 0, col < 2 * BLOCK])
    tables = jnp.where(edge[:, None, None, :], bias[None], NEG_INF)
    return jnp.swapaxes(tables, -1, -2) * LOG2_E


def _blockdiag_ones(n):
    idx = np.arange(n) // HEAD_DIM
    return jnp.asarray(idx[:, None] == idx[None, :], dtype=BF16)


def _whole(arr):
    zeros = (0,) * arr.ndim
    return pl.BlockSpec(arr.shape, lambda i: zeros, pipeline_mode=pl.Buffered(1))


def _layer(arr, layer):
    index = (layer,) + (0,) * (arr.ndim - 1)
    return pl.BlockSpec((None,) + arr.shape[1:], lambda i: index, pipeline_mode=pl.Buffered(1))


def _rows(width, tm=TILE_M):
    return pl.BlockSpec((tm, width), lambda i: (i, 0))


_PARAMS = pltpu.CompilerParams(dimension_semantics=("arbitrary",),
                               vmem_limit_bytes=VMEM_LIMIT_BYTES)


def _ffn_kvc(layer, x, g1, win, wout, gmix, wkvc, kg, ones):
    m, d = x.shape
    d_ff = wout.shape[1]
    params = (g1, win, wout, gmix, wkvc, kg)
    return pl.pallas_call(
        _ffn_kvc_kernel,
        grid=(m // TILE_M,),
        in_specs=[_rows(d)] + [_layer(a, layer) for a in params] + [_whole(ones)],
        out_specs=[_rows(d), _rows(KV_DIM), pl.BlockSpec((KV_DIM, TILE_M), lambda i: (0, i)),
                   _rows(CONV_DIM)],
        out_shape=[jax.ShapeDtypeStruct((m, d), F32), jax.ShapeDtypeStruct((m, KV_DIM), BF16),
                   jax.ShapeDtypeStruct((KV_DIM, m), BF16), jax.ShapeDtypeStruct((m, CONV_DIM), F32)],
        scratch_shapes=[pltpu.VMEM((TILE_M, d_ff), BF16)],
        compiler_params=_PARAMS,
        name="ffn_kvc",
    )(x, *params, ones)


def _ffn_ple(layer, x, g1, win, wout, gpe, wpg, p, wpp):
    m, d = x.shape
    d_ff = wout.shape[1]
    tm = TILE_M_PLE
    p_spec = pl.BlockSpec((None, tm, p.shape[2]), lambda i: (layer, i, 0))
    return pl.pallas_call(
        _ffn_ple_kernel,
        grid=(m // tm,),
        in_specs=[_rows(d, tm)] + [_layer(a, layer) for a in (g1, win, wout, gpe, wpg)]
                 + [p_spec, _layer(wpp, layer)],
        out_specs=_rows(d, tm),
        out_shape=jax.ShapeDtypeStruct((m, d), F32),
        scratch_shapes=[pltpu.VMEM((tm, d_ff), BF16)],
        compiler_params=_PARAMS,
        name="ffn_ple",
    )(x, g1, win, wout, gpe, wpg, p, wpp)


def _mix(layer, h, k, v, c, seq, gmix, wq, wg, qgain, ones, bias, sink, cw, cb, lng, lnb,
         wao, wco, wo):
    m, d = h.shape
    tm = TILE_M
    tiles_per_seq = seq // tm
    kv_blocks = tm // BLOCK
    last_kv = m // BLOCK - 1
    seg = tm // SUBLANES + 1
    reach = SUBLANES * seg - tm + CONV_PAD
    next_halo = next(n for n in (16, 32, 64, 128) if n >= reach)
    assert tm % next_halo == 0 and tm % CONV_HALO == 0

    def halo(rows, width, stride, last, side):
        if side < 0:
            return pl.BlockSpec((rows, width), lambda i: (jnp.maximum(i * stride - 1, 0), 0))
        return pl.BlockSpec((rows, width), lambda i: (jnp.minimum((i + 1) * stride, last), 0))

    k_specs = [halo(BLOCK, KV_DIM, kv_blocks, last_kv, -1), _rows(KV_DIM),
               halo(BLOCK, KV_DIM, kv_blocks, last_kv, +1)]
    v_specs = [pl.BlockSpec((KV_DIM, BLOCK), lambda i: (0, jnp.maximum(i * kv_blocks - 1, 0))),
               pl.BlockSpec((KV_DIM, tm), lambda i: (0, i)),
               pl.BlockSpec((KV_DIM, BLOCK), lambda i: (0, jnp.minimum((i + 1) * kv_blocks, last_kv)))]
    c_specs = [halo(CONV_HALO, CONV_DIM, tm // CONV_HALO, m // CONV_HALO - 1, -1), _rows(CONV_DIM),
               halo(next_halo, CONV_DIM, tm // next_halo, m // next_halo - 1, +1)]
    lay = lambda a: _layer(a, layer)
    return pl.pallas_call(
        functools.partial(_mix_kernel, tiles_per_seq),
        grid=(m // tm,),
        in_specs=[_rows(d)] + k_specs + v_specs + c_specs + [
            lay(gmix), lay(wq), lay(wg), lay(qgain), _whole(ones), _whole(bias), lay(sink),
            lay(cw), lay(cb), lay(lng), lay(lnb), lay(wao), lay(wco), lay(wo)],
        out_specs=_rows(d),
        out_shape=jax.ShapeDtypeStruct((m, d), F32),
        scratch_shapes=[pltpu.VMEM((tm + 2 * BLOCK, KV_DIM), BF16),
                        pltpu.VMEM((KV_DIM, tm + 2 * BLOCK), BF16),
                        pltpu.VMEM((CONV_DIM // LANES, CONV_HALO + tm + next_halo, LANES), F32),
                        pltpu.VMEM((CONV_DIM // LANES, SUBLANES * seg, LANES), F32),
                        pltpu.VMEM((tm, Q_DIM), BF16),
                        pltpu.VMEM((tm, CONV_DIM), BF16),
                        pltpu.VMEM((tm, 2 * d), F32)],
        compiler_params=_PARAMS,
        name="mix",
    )(h, k, k, k, v, v, v, c, c, c, gmix, wq, wg, qgain, ones, bias, sink, cw, cb, lng, lnb,
      wao, wco, wo)


def kernel(x, p, rel_bias, norm_ffn1, w_ffn1_in, w_ffn1_out, norm_mix, w_in, q_norm, k_norm, sink, conv_w, conv_b, conv_ln_g, conv_ln_b, w_attn_out, w_conv_out, w_o, norm_ffn2, w_ffn2_in, w_ffn2_out, norm_pe, w_pe_gate, w_pe_proj):
    batch, seq, d = x.shape
    depth = p.shape[0]
    m = batch * seq
    assert seq % TILE_M == 0 and TILE_M % BLOCK == 0 and seq // BLOCK >= 2

    kvc_lo, kvc_hi = Q_DIM, Q_DIM + 2 * KV_DIM + 2 * CONV_DIM
    w_q = w_in[:, :, :kvc_lo].astype(BF16)
    w_kvc = w_in[:, :, kvc_lo:kvc_hi].astype(BF16)
    w_g = w_in[:, :, kvc_hi:].astype(BF16)
    w1_in, w1_out = w_ffn1_in.astype(BF16), w_ffn1_out.astype(BF16)
    w2_in, w2_out = w_ffn2_in.astype(BF16), w_ffn2_out.astype(BF16)
    w_ao, w_co, w_oo = w_attn_out.astype(BF16), w_conv_out.astype(BF16), w_o.astype(BF16)
    w_pg, w_pp = w_pe_gate.astype(BF16), w_pe_proj.astype(BF16)

    bias = _bias_tables(rel_bias)
    ones_q, ones_k = _blockdiag_ones(Q_DIM), _blockdiag_ones(KV_DIM)
    vec = lambda a: a[:, None, :]
    q_gain = vec(jnp.tile(q_norm, (1, N_HEADS)) * (HEAD_DIM ** -0.5 * LOG2_E))
    k_gain = vec(jnp.tile(k_norm, (1, KV_HEADS)))
    sink_rows = jnp.repeat(sink.reshape(depth, KV_HEADS, 1, GROUP), BLOCK, axis=-1) * LOG2_E
    g_ffn1, g_mix, g_ffn2, g_pe = vec(norm_ffn1), vec(norm_mix), vec(norm_ffn2), vec(norm_pe)
    cb, lng, lnb = vec(conv_b), vec(conv_ln_g), vec(conv_ln_b)

    xf = x.reshape(m, d)
    pf = p.reshape(depth, m, p.shape[-1])
    for i in range(depth):
        h, k, v, c = _ffn_kvc(i, xf, g_ffn1, w1_in, w1_out, g_mix, w_kvc, k_gain, ones_k)
        h = _mix(i, h, k, v, c, seq, g_mix, w_q, w_g, q_gain, ones_q, bias, sink_rows,
                 conv_w, cb, lng, lnb, w_ao, w_co, w_oo)
        xf = _ffn_ple(i, h, g_ffn2, w2_in, w2_out, g_pe, w_pg, pf, w_pp)
    return xf.reshape(batch, seq, d)
```

```python
import functools
import math

import numpy as np
import jax
import jax.numpy as jnp
from jax import lax
from jax.experimental import pallas as pl
from jax.experimental.pallas import tpu as pltpu

N_HEADS = 8
KV_HEADS = 2
GROUP = N_HEADS // KV_HEADS
HEAD_DIM = 64
Q_DIM = N_HEADS * HEAD_DIM
KV_DIM = KV_HEADS * HEAD_DIM
BLOCK = 128
NUM_BUCKETS = 32
MAX_DISTANCE = 128
CONV_DIM = 512
CONV_WIDTH = 31
CONV_PAD = CONV_WIDTH // 2
CONV_HALO = 16
NEG_INF = -1e9
LOG2_E = math.log2(math.e)

TILE_M = 512
TILE_M_FFN = 1024
FF_CHUNK = 256
FFN_ROWS = 256
CONV_ROWS = 64
CONV_TILES = 13
LANES = 128
SUBLANES = 8
VMEM_LIMIT_BYTES = 56 * 1024 * 1024

F32 = jnp.float32
BF16 = jnp.bfloat16


def _rms(x, g, eps=1e-6):
    return x * lax.rsqrt(jnp.mean(x * x, axis=-1, keepdims=True) + eps) * g


def _dot(a, b):
    return jnp.dot(a, b, preferred_element_type=F32)


def _group_sumsq(x, ones_blockdiag):
    x2 = x * x
    hi = x2.astype(BF16)
    lo = (x2 - hi.astype(F32)).astype(BF16)
    return _dot(hi, ones_blockdiag) + _dot(lo, ones_blockdiag)


def _swiglu_half_step(x_ref, g_ref, win_ref, wout_ref, act_ref):
    d_ff = wout_ref.shape[0]
    groups = [slice(r0, r0 + FFN_ROWS) for r0 in range(0, x_ref.shape[0], FFN_ROWS)]
    for rows in groups:
        xn = _rms(x_ref[rows, :], g_ref[...]).astype(BF16)
        for lo in range(0, d_ff, FF_CHUNK):
            gate = _dot(xn, win_ref[:, lo:lo + FF_CHUNK])
            up = _dot(xn, win_ref[:, d_ff + lo:d_ff + lo + FF_CHUNK])
            act_ref[rows, lo:lo + FF_CHUNK] = (gate * jax.nn.sigmoid(gate) * up).astype(BF16)
    return [(rows, x_ref[rows, :] + 0.5 * _dot(act_ref[rows, :], wout_ref[...])) for rows in groups]


def _ffn_kvc_kernel(x_ref, g1_ref, win_ref, wout_ref, gmix_ref, wkvc_ref, kg_ref, ones_ref,
                    h_ref, k_ref, v_ref, c_ref, act_ref):
    for rows, h in _swiglu_half_step(x_ref, g1_ref, win_ref, wout_ref, act_ref):
        h_ref[rows, :] = h
        u = _rms(h, gmix_ref[...]).astype(BF16)
        kvc = _dot(u, wkvc_ref[...])
        k = kvc[:, :KV_DIM]
        ss = _group_sumsq(k, ones_ref[...])
        k_ref[rows, :] = (k * lax.rsqrt(ss * (1.0 / HEAD_DIM) + 1e-6) * kg_ref[...]).astype(BF16)
        v_ref[:, rows] = kvc[:, KV_DIM:2 * KV_DIM].T.astype(BF16)
        c_val = kvc[:, 2 * KV_DIM:2 * KV_DIM + CONV_DIM]
        c_gate = kvc[:, 2 * KV_DIM + CONV_DIM:]
        c_ref[rows, :] = c_val * jax.nn.sigmoid(c_gate)


def _ffn_ple_kernel(x_ref, g1_ref, win_ref, wout_ref, gpe_ref, wpg_ref, p_ref, wpp_ref,
                    o_ref, act_ref):
    for rows, h in _swiglu_half_step(x_ref, g1_ref, win_ref, wout_ref, act_ref):
        t = _rms(h, gpe_ref[...]).astype(BF16)
        gate = jax.nn.sigmoid(_dot(t, wpg_ref[...]))
        pe = _dot(p_ref[rows, :].astype(BF16), wpp_ref[...])
        o_ref[rows, :] = h + pe * gate


def _mix_kernel(tiles_per_seq,
                h_ref, kp_ref, kc_ref, kn_ref, vp_ref, vc_ref, vn_ref, cp_ref, cc_ref, cn_ref,
                gmix_ref, wq_ref, wg_ref, qgain_ref, ones_ref, bias_ref, sink_ref,
                cw_ref, cb_ref, lng_ref, lnb_ref, wao_ref, wco_ref, wo_ref,
                o_ref, kext_ref, vext_ref, cext_ref, conv_ref, y_ref, cact_ref, gate_ref):
    tm, d = h_ref.shape
    n_blocks = tm // BLOCK
    tile = pl.program_id(0) % tiles_per_seq
    is_first = tile == 0
    is_last = tile == tiles_per_seq - 1

    h = h_ref[...]
    u = _rms(h, gmix_ref[...]).astype(BF16)
    q = _dot(u, wq_ref[...])
    ss = _group_sumsq(q, ones_ref[...])
    qn = (q * lax.rsqrt(ss * (1.0 / HEAD_DIM) + 1e-6) * qgain_ref[...]).astype(BF16)

    lane_groups = CONV_DIM // LANES
    seg = conv_ref.shape[1] // SUBLANES
    assert seg % 2 == 1 and seg * SUBLANES >= tm and seg % CONV_TILES == 0
    for g in range(lane_groups):
        lanes = slice(g * LANES, (g + 1) * LANES)
        cext_ref[g, :CONV_HALO] = jnp.where(is_first, 0.0, cp_ref[:, lanes])
        cext_ref[g, CONV_HALO:CONV_HALO + tm] = cc_ref[:, lanes]
        cext_ref[g, CONV_HALO + tm:] = jnp.where(is_last, 0.0, cn_ref[:, lanes])
    first_tap = CONV_HALO - CONV_PAD
    def conv_block(g, j0):
        lanes = slice(g * LANES, (g + 1) * LANES)
        windows = {}

        def window(j):
            if j not in windows:
                windows[j] = cext_ref[g, pl.ds(first_tap + j, SUBLANES, stride=seg), :]
            return windows[j]

        accs = [jnp.zeros((SUBLANES, LANES), F32)] * CONV_TILES
        for w in range(CONV_WIDTH):
            tap = cw_ref[w:w + 1, lanes]
            accs = [acc + window(j0 + t + w) * tap for t, acc in enumerate(accs)]
        for t, acc in enumerate(accs):
            conv_ref[g, pl.ds(j0 + t, SUBLANES, stride=seg), :] = acc + cb_ref[:, lanes]

    for g in range(lane_groups):
        for j0 in range(0, seg, CONV_TILES):
            conv_block(g, j0)
    for r0 in range(0, tm, CONV_ROWS):
        c = jnp.concatenate([conv_ref[g, r0:r0 + CONV_ROWS, :] for g in range(lane_groups)], axis=1)
        mu = jnp.mean(c, axis=-1, keepdims=True)
        var = jnp.mean(jnp.square(c - mu), axis=-1, keepdims=True)
        c = (c - mu) * lax.rsqrt(var + 1e-5) * lng_ref[...] + lnb_ref[...]
        cact_ref[r0:r0 + CONV_ROWS, :] = (c * jax.nn.sigmoid(c)).astype(BF16)
    y_conv = _dot(cact_ref[...], wco_ref[...])

    n_chunks = tm // CONV_ROWS
    gate_cols = gate_ref.shape[1] // n_chunks

    def gate_chunk(ci):
        g0 = ci * gate_cols
        gate_ref[:, g0:g0 + gate_cols] = jax.nn.sigmoid(_dot(u, wg_ref[:, g0:g0 + gate_cols]))

    kext_ref[:BLOCK] = kp_ref[...]
    kext_ref[BLOCK:BLOCK + tm] = kc_ref[...]
    kext_ref[BLOCK + tm:] = kn_ref[...]
    vext_ref[:, :BLOCK] = vp_ref[...]
    vext_ref[:, BLOCK:BLOCK + tm] = vc_ref[...]
    vext_ref[:, BLOCK + tm:] = vn_ref[...]

    def scores(r, kv):
        if r == 0:
            table = jnp.where(is_first, 0, 1)
        elif r == n_blocks - 1:
            table = jnp.where(is_last, 2, 1)
        else:
            table = 1
        rows = slice(r * BLOCK, (r + 1) * BLOCK)
        q4 = jnp.concatenate(
            [qn[rows, (kv * GROUP + j) * HEAD_DIM:(kv * GROUP + j + 1) * HEAD_DIM]
             for j in range(GROUP)], axis=0)
        kw = kext_ref[r * BLOCK:(r + 3) * BLOCK, kv * HEAD_DIM:(kv + 1) * HEAD_DIM]
        return lax.dot_general(kw, q4, (((1,), (1,)), ((), ())),
                               preferred_element_type=F32) + bias_ref[table, kv]

    def softmax(s, kv):
        m = jnp.max(s, axis=0, keepdims=True)
        e = jnp.exp2(s - m)
        denom = jnp.sum(e, axis=0, keepdims=True) + jnp.exp2(sink_ref[kv] - m)
        return e.astype(BF16), 1.0 / denom

    def weighted_values(e, inv_denom, r, kv):
        vwt = vext_ref[kv * HEAD_DIM:(kv + 1) * HEAD_DIM, r * BLOCK:(r + 3) * BLOCK]
        o = _dot(vwt, e) * inv_denom
        for pair in range(GROUP // 2):
            two = jnp.concatenate([o[:, (2 * pair) * BLOCK:(2 * pair + 1) * BLOCK],
                                   o[:, (2 * pair + 1) * BLOCK:(2 * pair + 2) * BLOCK]], axis=0)
            col = (kv * GROUP + 2 * pair) * HEAD_DIM
            y_ref[r * BLOCK:(r + 1) * BLOCK, col:col + 2 * HEAD_DIM] = two.T.astype(BF16)

    units = [(r, kv) for r in range(n_blocks) for kv in range(KV_HEADS)]
    s_of, p_of = {}, {}
    for t in range(len(units) + 2):
        if t < len(units):
            s_of[t] = scores(*units[t])
        if t < n_chunks:
            gate_chunk(t)
        if 0 <= t - 1 < len(units):
            p_of[t - 1] = softmax(s_of.pop(t - 1), units[t - 1][1])
        if 0 <= t - 2 < len(units):
            weighted_values(*p_of.pop(t - 2), *units[t - 2])
    y_attn = _dot(y_ref[...], wao_ref[...])

    merged = gate_ref[:, :d] * y_attn + gate_ref[:, d:] * y_conv
    o_ref[...] = h + _dot(merged.astype(BF16), wo_ref[...])


def _bias_tables(rel_bias):
    half = NUM_BUCKETS // 2
    max_exact = half // 2
    period = 4 * BLOCK
    k = np.arange(period)
    rel = np.where(k < 3 * BLOCK, k - BLOCK, k - BLOCK - period)
    n = np.abs(rel)
    nf = np.maximum(n, 1).astype(np.float64)
    large = max_exact + (np.log(nf / max_exact) / math.log(MAX_DISTANCE / max_exact)
                         * (half - max_exact)).astype(np.int32)
    buckets = np.where(rel > 0, half, 0) + np.where(n < max_exact, n, np.minimum(large, half - 1))
    vec = jnp.where((n <= BLOCK)[None, :], rel_bias.astype(F32)[buckets].T, NEG_INF)
    skew = jnp.tile(vec, (1, BLOCK))[:, :BLOCK * (period - 1)].reshape(N_HEADS, BLOCK, period - 1)
    bias = skew[:, :, :3 * BLOCK].reshape(KV_HEADS, GROUP * BLOCK, 3 * BLOCK)
    col = np.arange(3 * BLOCK)
    edge = np.stack([col >= BLOCK, col >= 0, col < 2 * BLOCK])
    tables = jnp.where(edge[:, None, None, :], bias[None], NEG_INF)
    return jnp.swapaxes(tables, -1, -2) * LOG2_E


def _blockdiag_ones(n):
    idx = np.arange(n) // HEAD_DIM
    return jnp.asarray(idx[:, None] == idx[None, :], dtype=BF16)


def _whole(arr):
    zeros = (0,) * arr.ndim
    return pl.BlockSpec(arr.shape, lambda i: zeros, pipeline_mode=pl.Buffered(1))


def _layer(arr, layer):
    index = (layer,) + (0,) * (arr.ndim - 1)
    return pl.BlockSpec((None,) + arr.shape[1:], lambda i: index, pipeline_mode=pl.Buffered(1))


def _rows(width, tm=TILE_M):
    return pl.BlockSpec((tm, width), lambda i: (i, 0))


_PARAMS = pltpu.CompilerParams(dimension_semantics=("arbitrary",),
                               vmem_limit_bytes=VMEM_LIMIT_BYTES)


def _ffn_kvc(layer, x, g1, win, wout, gmix, wkvc, kg, ones):
    m, d = x.shape
    d_ff = wout.shape[1]
    params = (g1, win, wout, gmix, wkvc, kg)
    tm = TILE_M_FFN
    return pl.pallas_call(
        _ffn_kvc_kernel,
        grid=(m // tm,),
        in_specs=[_rows(d, tm)] + [_layer(a, layer) for a in params] + [_whole(ones)],
        out_specs=[_rows(d, tm), _rows(KV_DIM, tm), pl.BlockSpec((KV_DIM, tm), lambda i: (0, i)),
                   _rows(CONV_DIM, tm)],
        out_shape=[jax.ShapeDtypeStruct((m, d), F32), jax.ShapeDtypeStruct((m, KV_DIM), BF16),
                   jax.ShapeDtypeStruct((KV_DIM, m), BF16), jax.ShapeDtypeStruct((m, CONV_DIM), F32)],
        scratch_shapes=[pltpu.VMEM((tm, d_ff), BF16)],
        compiler_params=_PARAMS,
        name="ffn_kvc",
    )(x, *params, ones)


def _ffn_ple(layer, x, g1, win, wout, gpe, wpg, p, wpp):
    m, d = x.shape
    d_ff = wout.shape[1]
    tm = TILE_M_FFN
    p_spec = pl.BlockSpec((None, tm, p.shape[2]), lambda i: (layer, i, 0))
    return pl.pallas_call(
        _ffn_ple_kernel,
        grid=(m // tm,),
        in_specs=[_rows(d, tm)] + [_layer(a, layer) for a in (g1, win, wout, gpe, wpg)]
                 + [p_spec, _layer(wpp, layer)],
        out_specs=_rows(d, tm),
        out_shape=jax.ShapeDtypeStruct((m, d), F32),
        scratch_shapes=[pltpu.VMEM((tm, d_ff), BF16)],
        compiler_params=_PARAMS,
        name="ffn_ple",
    )(x, g1, win, wout, gpe, wpg, p, wpp)


def _mix(layer, h, k, v, c, seq, gmix, wq, wg, qgain, ones, bias, sink, cw, cb, lng, lnb,
         wao, wco, wo):
    m, d = h.shape
    tm = TILE_M
    tiles_per_seq = seq // tm
    kv_blocks = tm // BLOCK
    last_kv = m // BLOCK - 1
    seg = tm // SUBLANES + 1
    reach = SUBLANES * seg - tm + CONV_PAD
    next_halo = next(n for n in (16, 32, 64, 128) if n >= reach)
    assert tm % next_halo == 0 and tm % CONV_HALO == 0

    def halo(rows, width, stride, last, side):
        if side < 0:
            return pl.BlockSpec((rows, width), lambda i: (jnp.maximum(i * stride - 1, 0), 0))
        return pl.BlockSpec((rows, width), lambda i: (jnp.minimum((i + 1) * stride, last), 0))

    k_specs = [halo(BLOCK, KV_DIM, kv_blocks, last_kv, -1), _rows(KV_DIM),
               halo(BLOCK, KV_DIM, kv_blocks, last_kv, +1)]
    v_specs = [pl.BlockSpec((KV_DIM, BLOCK), lambda i: (0, jnp.maximum(i * kv_blocks - 1, 0))),
               pl.BlockSpec((KV_DIM, tm), lambda i: (0, i)),
               pl.BlockSpec((KV_DIM, BLOCK), lambda i: (0, jnp.minimum((i + 1) * kv_blocks, last_kv)))]
    c_specs = [halo(CONV_HALO, CONV_DIM, tm // CONV_HALO, m // CONV_HALO - 1, -1), _rows(CONV_DIM),
               halo(next_halo, CONV_DIM, tm // next_halo, m // next_halo - 1, +1)]
    lay = lambda a: _layer(a, layer)
    return pl.pallas_call(
        functools.partial(_mix_kernel, tiles_per_seq),
        grid=(m // tm,),
        in_specs=[_rows(d)] + k_specs + v_specs + c_specs + [
            lay(gmix), lay(wq), lay(wg), lay(qgain), _whole(ones), _whole(bias), lay(sink),
            lay(cw), lay(cb), lay(lng), lay(lnb), lay(wao), lay(wco), lay(wo)],
        out_specs=_rows(d),
        out_shape=jax.ShapeDtypeStruct((m, d), F32),
        scratch_shapes=[pltpu.VMEM((tm + 2 * BLOCK, KV_DIM), BF16),
                        pltpu.VMEM((KV_DIM, tm + 2 * BLOCK), BF16),
                        pltpu.VMEM((CONV_DIM // LANES, CONV_HALO + tm + next_halo, LANES), F32),
                        pltpu.VMEM((CONV_DIM // LANES, SUBLANES * seg, LANES), F32),
                        pltpu.VMEM((tm, Q_DIM), BF16),
                        pltpu.VMEM((tm, CONV_DIM), BF16),
                        pltpu.VMEM((tm, 2 * d), F32)],
        compiler_params=_PARAMS,
        name="mix",
    )(h, k, k, k, v, v, v, c, c, c, gmix, wq, wg, qgain, ones, bias, sink, cw, cb, lng, lnb,
      wao, wco, wo)


def kernel(x, p, rel_bias, norm_ffn1, w_ffn1_in, w_ffn1_out, norm_mix, w_in, q_norm, k_norm, sink, conv_w, conv_b, conv_ln_g, conv_ln_b, w_attn_out, w_conv_out, w_o, norm_ffn2, w_ffn2_in, w_ffn2_out, norm_pe, w_pe_gate, w_pe_proj):
    batch, seq, d = x.shape
    depth = p.shape[0]
    m = batch * seq
    assert seq % TILE_M == 0 and TILE_M % BLOCK == 0 and seq // BLOCK >= 2
    assert m % TILE_M_FFN == 0 and TILE_M_FFN % FFN_ROWS == 0

    kvc_lo, kvc_hi = Q_DIM, Q_DIM + 2 * KV_DIM + 2 * CONV_DIM
    w_q = w_in[:, :, :kvc_lo].astype(BF16)
    w_kvc = w_in[:, :, kvc_lo:kvc_hi].astype(BF16)
    w_g = w_in[:, :, kvc_hi:].astype(BF16)
    w1_in, w1_out = w_ffn1_in.astype(BF16), w_ffn1_out.astype(BF16)
    w2_in, w2_out = w_ffn2_in.astype(BF16), w_ffn2_out.astype(BF16)
    w_ao, w_co, w_oo = w_attn_out.astype(BF16), w_conv_out.astype(BF16), w_o.astype(BF16)
    w_pg, w_pp = w_pe_gate.astype(BF16), w_pe_proj.astype(BF16)

    bias = _bias_tables(rel_bias)
    ones_q, ones_k = _blockdiag_ones(Q_DIM), _blockdiag_ones(KV_DIM)
    vec = lambda a: a[:, None, :]
    q_gain = vec(jnp.tile(q_norm, (1, N_HEADS)) * (HEAD_DIM ** -0.5 * LOG2_E))
    k_gain = vec(jnp.tile(k_norm, (1, KV_HEADS)))
    sink_rows = jnp.repeat(sink.reshape(depth, KV_HEADS, 1, GROUP), BLOCK, axis=-1) * LOG2_E
    g_ffn1, g_mix, g_ffn2, g_pe = vec(norm_ffn1), vec(norm_mix), vec(norm_ffn2), vec(norm_pe)
    cb, lng, lnb = vec(conv_b), vec(conv_ln_g), vec(conv_ln_b)

    xf = x.reshape(m, d)
    pf = p.reshape(depth, m, p.shape[-1])
    for i in range(depth):
        h, k, v, c = _ffn_kvc(i, xf, g_ffn1, w1_in, w1_out, g_mix, w_kvc, k_gain, ones_k)
        h = _mix(i, h, k, v, c, seq, g_mix, w_q, w_g, q_gain, ones_q, bias, sink_rows,
                 conv_w, cb, lng, lnb, w_ao, w_co, w_oo)
        xf = _ffn_ple(i, h, g_ffn2, w2_in, w2_out, g_pe, w_pg, pf, w_pp)
    return xf.reshape(batch, seq, d)
```

```python
import functools
import math

import numpy as np
import jax
import jax.numpy as jnp
from jax import lax
from jax.experimental import pallas as pl
from jax.experimental.pallas import tpu as pltpu

N_HEADS = 8
KV_HEADS = 2
GROUP = N_HEADS // KV_HEADS
HEAD_DIM = 64
Q_DIM = N_HEADS * HEAD_DIM
KV_DIM = KV_HEADS * HEAD_DIM
BLOCK = 128
NUM_BUCKETS = 32
MAX_DISTANCE = 128
CONV_DIM = 512
CONV_WIDTH = 31
CONV_PAD = CONV_WIDTH // 2
CONV_HALO = 16
NEG_INF = -1e9
LOG2_E = math.log2(math.e)

TILE_M = 512
TILE_M_FFN = 1024
FF_CHUNK = 256
FFN_ROWS = 256
CONV_ROWS = 64
CONV_TILES = 13
LANES = 128
SUBLANES = 8
VMEM_LIMIT_BYTES = 56 * 1024 * 1024

F32 = jnp.float32
BF16 = jnp.bfloat16


def _rms(x, g, eps=1e-6):
    return x * lax.rsqrt(jnp.mean(x * x, axis=-1, keepdims=True) + eps) * g


def _dot(a, b):
    return jnp.dot(a, b, preferred_element_type=F32)


def _group_sumsq(x, ones_blockdiag):
    x2 = x * x
    hi = x2.astype(BF16)
    lo = (x2 - hi.astype(F32)).astype(BF16)
    return _dot(hi, ones_blockdiag) + _dot(lo, ones_blockdiag)


def _swiglu_half_step(x_ref, g_ref, win_ref, wout_ref, act_ref):
    d_ff = wout_ref.shape[0]
    groups = [slice(r0, r0 + FFN_ROWS) for r0 in range(0, x_ref.shape[0], FFN_ROWS)]
    for rows in groups:
        xn = _rms(x_ref[rows, :], g_ref[...]).astype(BF16)
        for lo in range(0, d_ff, FF_CHUNK):
            gate = _dot(xn, win_ref[:, lo:lo + FF_CHUNK])
            up = _dot(xn, win_ref[:, d_ff + lo:d_ff + lo + FF_CHUNK])
            act_ref[rows, lo:lo + FF_CHUNK] = (gate * jax.nn.sigmoid(gate) * up).astype(BF16)
    return [(rows, x_ref[rows, :] + 0.5 * _dot(act_ref[rows, :], wout_ref[...])) for rows in groups]


def _ffn_qkvc_kernel(x_ref, g1_ref, win_ref, wout_ref, gmix_ref, wqkvc_ref, qg_ref, kg_ref, ones_ref,
                     h_ref, q_ref, k_ref, v_ref, c_ref, act_ref):
    def qk_norm(t, gain_ref):
        ss = _group_sumsq(t, ones_ref[:t.shape[1], :t.shape[1]])
        return (t * lax.rsqrt(ss * (1.0 / HEAD_DIM) + 1e-6) * gain_ref[...]).astype(BF16)

    for rows, h in _swiglu_half_step(x_ref, g1_ref, win_ref, wout_ref, act_ref):
        h_ref[rows, :] = h
        u = _rms(h, gmix_ref[...]).astype(BF16)
        qkvc = _dot(u, wqkvc_ref[...])
        q_ref[rows, :] = qk_norm(qkvc[:, :Q_DIM], qg_ref)
        k_ref[rows, :] = qk_norm(qkvc[:, Q_DIM:Q_DIM + KV_DIM], kg_ref)
        v_ref[:, rows] = qkvc[:, Q_DIM + KV_DIM:Q_DIM + 2 * KV_DIM].T.astype(BF16)
        c_val = qkvc[:, Q_DIM + 2 * KV_DIM:Q_DIM + 2 * KV_DIM + CONV_DIM]
        c_gate = qkvc[:, Q_DIM + 2 * KV_DIM + CONV_DIM:]
        c_ref[rows, :] = c_val * jax.nn.sigmoid(c_gate)


def _ffn_ple_kernel(x_ref, g1_ref, win_ref, wout_ref, gpe_ref, wpg_ref, p_ref, wpp_ref,
                    o_ref, act_ref):
    for rows, h in _swiglu_half_step(x_ref, g1_ref, win_ref, wout_ref, act_ref):
        t = _rms(h, gpe_ref[...]).astype(BF16)
        gate = jax.nn.sigmoid(_dot(t, wpg_ref[...]))
        pe = _dot(p_ref[rows, :].astype(BF16), wpp_ref[...])
        o_ref[rows, :] = h + pe * gate


def _mix_kernel(tiles_per_seq,
                h_ref, q_ref, kp_ref, kc_ref, kn_ref, vp_ref, vc_ref, vn_ref, cp_ref, cc_ref, cn_ref,
                gmix_ref, wg_ref, bias_ref, sink_ref,
                cw_ref, cb_ref, lng_ref, lnb_ref, wao_ref, wco_ref, wo_ref,
                o_ref, kext_ref, vext_ref, cext_ref, conv_ref, y_ref, cact_ref, gate_ref):
    tm, d = h_ref.shape
    n_blocks = tm // BLOCK
    tile = pl.program_id(0) % tiles_per_seq
    is_first = tile == 0
    is_last = tile == tiles_per_seq - 1

    h = h_ref[...]
    u = _rms(h, gmix_ref[...]).astype(BF16)

    lane_groups = CONV_DIM // LANES
    seg = conv_ref.shape[1] // SUBLANES
    assert seg % 2 == 1 and seg * SUBLANES >= tm and seg % CONV_TILES == 0
    for g in range(lane_groups):
        lanes = slice(g * LANES, (g + 1) * LANES)
        cext_ref[g, :CONV_HALO] = jnp.where(is_first, 0.0, cp_ref[:, lanes])
        cext_ref[g, CONV_HALO:CONV_HALO + tm] = cc_ref[:, lanes]
        cext_ref[g, CONV_HALO + tm:] = jnp.where(is_last, 0.0, cn_ref[:, lanes])
    first_tap = CONV_HALO - CONV_PAD
    def conv_block(g, j0):
        lanes = slice(g * LANES, (g + 1) * LANES)
        windows = {}

        def window(j):
            if j not in windows:
                windows[j] = cext_ref[g, pl.ds(first_tap + j, SUBLANES, stride=seg), :]
            return windows[j]

        accs = [jnp.zeros((SUBLANES, LANES), F32)] * CONV_TILES
        for w in range(CONV_WIDTH):
            tap = cw_ref[w:w + 1, lanes]
            accs = [acc + window(j0 + t + w) * tap for t, acc in enumerate(accs)]
        for t, acc in enumerate(accs):
            conv_ref[g, pl.ds(j0 + t, SUBLANES, stride=seg), :] = acc + cb_ref[:, lanes]

    for g in range(lane_groups):
        for j0 in range(0, seg, CONV_TILES):
            conv_block(g, j0)
    for r0 in range(0, tm, CONV_ROWS):
        c = jnp.concatenate([conv_ref[g, r0:r0 + CONV_ROWS, :] for g in range(lane_groups)], axis=1)
        mu = jnp.mean(c, axis=-1, keepdims=True)
        var = jnp.mean(jnp.square(c - mu), axis=-1, keepdims=True)
        c = (c - mu) * lax.rsqrt(var + 1e-5) * lng_ref[...] + lnb_ref[...]
        cact_ref[r0:r0 + CONV_ROWS, :] = (c * jax.nn.sigmoid(c)).astype(BF16)
    y_conv = _dot(cact_ref[...], wco_ref[...])

    n_chunks = tm // CONV_ROWS
    gate_cols = gate_ref.shape[1] // n_chunks

    def gate_chunk(ci):
        g0 = ci * gate_cols
        gate_ref[:, g0:g0 + gate_cols] = jax.nn.sigmoid(_dot(u, wg_ref[:, g0:g0 + gate_cols]))

    kext_ref[:BLOCK] = kp_ref[...]
    kext_ref[BLOCK:BLOCK + tm] = kc_ref[...]
    kext_ref[BLOCK + tm:] = kn_ref[...]
    vext_ref[:, :BLOCK] = vp_ref[...]
    vext_ref[:, BLOCK:BLOCK + tm] = vc_ref[...]
    vext_ref[:, BLOCK + tm:] = vn_ref[...]

    def scores(r, kv):
        if r == 0:
            table = jnp.where(is_first, 0, 1)
        elif r == n_blocks - 1:
            table = jnp.where(is_last, 2, 1)
        else:
            table = 1
        rows = slice(r * BLOCK, (r + 1) * BLOCK)
        q4 = jnp.concatenate(
            [q_ref[rows, (kv * GROUP + j) * HEAD_DIM:(kv * GROUP + j + 1) * HEAD_DIM]
             for j in range(GROUP)], axis=0)
        kw = kext_ref[r * BLOCK:(r + 3) * BLOCK, kv * HEAD_DIM:(kv + 1) * HEAD_DIM]
        return lax.dot_general(kw, q4, (((1,), (1,)), ((), ())),
                               preferred_element_type=F32) + bias_ref[table, kv]

    def softmax(s, kv):
        m = jnp.max(s, axis=0, keepdims=True)
        e = jnp.exp2(s - m)
        denom = jnp.sum(e, axis=0, keepdims=True) + jnp.exp2(sink_ref[kv] - m)
        return e.astype(BF16), 1.0 / denom

    def weighted_values(e, inv_denom, r, kv):
        vwt = vext_ref[kv * HEAD_DIM:(kv + 1) * HEAD_DIM, r * BLOCK:(r + 3) * BLOCK]
        o = _dot(vwt, e) * inv_denom
        for pair in range(GROUP // 2):
            two = jnp.concatenate([o[:, (2 * pair) * BLOCK:(2 * pair + 1) * BLOCK],
                                   o[:, (2 * pair + 1) * BLOCK:(2 * pair + 2) * BLOCK]], axis=0)
            col = (kv * GROUP + 2 * pair) * HEAD_DIM
            y_ref[r * BLOCK:(r + 1) * BLOCK, col:col + 2 * HEAD_DIM] = two.T.astype(BF16)

    units = [(r, kv) for r in range(n_blocks) for kv in range(KV_HEADS)]
    s_of, p_of = {}, {}
    for t in range(len(units) + 2):
        if t < len(units):
            s_of[t] = scores(*units[t])
        if t < n_chunks:
            gate_chunk(t)
        if 0 <= t - 1 < len(units):
            p_of[t - 1] = softmax(s_of.pop(t - 1), units[t - 1][1])
        if 0 <= t - 2 < len(units):
            weighted_values(*p_of.pop(t - 2), *units[t - 2])
    y_attn = _dot(y_ref[...], wao_ref[...])

    merged = gate_ref[:, :d] * y_attn + gate_ref[:, d:] * y_conv
    o_ref[...] = h + _dot(merged.astype(BF16), wo_ref[...])


def _bias_tables(rel_bias):
    half = NUM_BUCKETS // 2
    max_exact = half // 2
    period = 4 * BLOCK
    k = np.arange(period)
    rel = np.where(k < 3 * BLOCK, k - BLOCK, k - BLOCK - period)
    n = np.abs(rel)
    nf = np.maximum(n, 1).astype(np.float64)
    large = max_exact + (np.log(nf / max_exact) / math.log(MAX_DISTANCE / max_exact)
                         * (half - max_exact)).astype(np.int32)
    buckets = np.where(rel > 0, half, 0) + np.where(n < max_exact, n, np.minimum(large, half - 1))
    vec = jnp.where((n <= BLOCK)[None, :], rel_bias.astype(F32)[buckets].T, NEG_INF)
    skew = jnp.tile(vec, (1, BLOCK))[:, :BLOCK * (period - 1)].reshape(N_HEADS, BLOCK, period - 1)
    bias = skew[:, :, :3 * BLOCK].reshape(KV_HEADS, GROUP * BLOCK, 3 * BLOCK)
    col = np.arange(3 * BLOCK)
    edge = np.stack([col >= BLOCK, col >= 0, col < 2 * BLOCK])
    tables = jnp.where(edge[:, None, None, :], bias[None], NEG_INF)
    return jnp.swapaxes(tables, -1, -2) * LOG2_E


def _blockdiag_ones(n):
    idx = np.arange(n) // HEAD_DIM
    return jnp.asarray(idx[:, None] == idx[None, :], dtype=BF16)


def _whole(arr):
    zeros = (0,) * arr.ndim
    return pl.BlockSpec(arr.shape, lambda i: zeros, pipeline_mode=pl.Buffered(1))


def _layer(arr, layer):
    index = (layer,) + (0,) * (arr.ndim - 1)
    return pl.BlockSpec((None,) + arr.shape[1:], lambda i: index, pipeline_mode=pl.Buffered(1))


def _rows(width, tm=TILE_M):
    return pl.BlockSpec((tm, width), lambda i: (i, 0))


_PARAMS = pltpu.CompilerParams(dimension_semantics=("arbitrary",),
                               vmem_limit_bytes=VMEM_LIMIT_BYTES)


def _ffn_qkvc(layer, x, g1, win, wout, gmix, wqkvc, qg, kg, ones):
    m, d = x.shape
    d_ff = wout.shape[1]
    params = (g1, win, wout, gmix, wqkvc, qg, kg)
    tm = TILE_M_FFN
    return pl.pallas_call(
        _ffn_qkvc_kernel,
        grid=(m // tm,),
        in_specs=[_rows(d, tm)] + [_layer(a, layer) for a in params] + [_whole(ones)],
        out_specs=[_rows(d, tm), _rows(Q_DIM, tm), _rows(KV_DIM, tm),
                   pl.BlockSpec((KV_DIM, tm), lambda i: (0, i)), _rows(CONV_DIM, tm)],
        out_shape=[jax.ShapeDtypeStruct((m, d), F32), jax.ShapeDtypeStruct((m, Q_DIM), BF16),
                   jax.ShapeDtypeStruct((m, KV_DIM), BF16), jax.ShapeDtypeStruct((KV_DIM, m), BF16),
                   jax.ShapeDtypeStruct((m, CONV_DIM), F32)],
        scratch_shapes=[pltpu.VMEM((tm, d_ff), BF16)],
        compiler_params=_PARAMS,
        name="ffn_qkvc",
    )(x, *params, ones)


def _ffn_ple(layer, x, g1, win, wout, gpe, wpg, p, wpp):
    m, d = x.shape
    d_ff = wout.shape[1]
    tm = TILE_M_FFN
    p_spec = pl.BlockSpec((None, tm, p.shape[2]), lambda i: (layer, i, 0))
    return pl.pallas_call(
        _ffn_ple_kernel,
        grid=(m // tm,),
        in_specs=[_rows(d, tm)] + [_layer(a, layer) for a in (g1, win, wout, gpe, wpg)]
                 + [p_spec, _layer(wpp, layer)],
        out_specs=_rows(d, tm),
        out_shape=jax.ShapeDtypeStruct((m, d), F32),
        scratch_shapes=[pltpu.VMEM((tm, d_ff), BF16)],
        compiler_params=_PARAMS,
        name="ffn_ple",
    )(x, g1, win, wout, gpe, wpg, p, wpp)


def _mix(layer, h, q, k, v, c, seq, gmix, wg, bias, sink, cw, cb, lng, lnb, wao, wco, wo):
    m, d = h.shape
    tm = TILE_M
    tiles_per_seq = seq // tm
    kv_blocks = tm // BLOCK
    last_kv = m // BLOCK - 1
    seg = tm // SUBLANES + 1
    reach = SUBLANES * seg - tm + CONV_PAD
    next_halo = next(n for n in (16, 32, 64, 128) if n >= reach)
    assert tm % next_halo == 0 and tm % CONV_HALO == 0

    def halo(rows, width, stride, last, side):
        if side < 0:
            return pl.BlockSpec((rows, width), lambda i: (jnp.maximum(i * stride - 1, 0), 0))
        return pl.BlockSpec((rows, width), lambda i: (jnp.minimum((i + 1) * stride, last), 0))

    k_specs = [halo(BLOCK, KV_DIM, kv_blocks, last_kv, -1), _rows(KV_DIM),
               halo(BLOCK, KV_DIM, kv_blocks, last_kv, +1)]
    v_specs = [pl.BlockSpec((KV_DIM, BLOCK), lambda i: (0, jnp.maximum(i * kv_blocks - 1, 0))),
               pl.BlockSpec((KV_DIM, tm), lambda i: (0, i)),
               pl.BlockSpec((KV_DIM, BLOCK), lambda i: (0, jnp.minimum((i + 1) * kv_blocks, last_kv)))]
    c_specs = [halo(CONV_HALO, CONV_DIM, tm // CONV_HALO, m // CONV_HALO - 1, -1), _rows(CONV_DIM),
               halo(next_halo, CONV_DIM, tm // next_halo, m // next_halo - 1, +1)]
    lay = lambda a: _layer(a, layer)
    return pl.pallas_call(
        functools.partial(_mix_kernel, tiles_per_seq),
        grid=(m // tm,),
        in_specs=[_rows(d), _rows(Q_DIM)] + k_specs + v_specs + c_specs + [
            lay(gmix), lay(wg), _whole(bias), lay(sink),
            lay(cw), lay(cb), lay(lng), lay(lnb), lay(wao), lay(wco), lay(wo)],
        out_specs=_rows(d),
        out_shape=jax.ShapeDtypeStruct((m, d), F32),
        scratch_shapes=[pltpu.VMEM((tm + 2 * BLOCK, KV_DIM), BF16),
                        pltpu.VMEM((KV_DIM, tm + 2 * BLOCK), BF16),
                        pltpu.VMEM((CONV_DIM // LANES, CONV_HALO + tm + next_halo, LANES), F32),
                        pltpu.VMEM((CONV_DIM // LANES, SUBLANES * seg, LANES), F32),
                        pltpu.VMEM((tm, Q_DIM), BF16),
                        pltpu.VMEM((tm, CONV_DIM), BF16),
                        pltpu.VMEM((tm, 2 * d), F32)],
        compiler_params=_PARAMS,
        name="mix",
    )(h, q, k, k, k, v, v, v, c, c, c, gmix, wg, bias, sink, cw, cb, lng, lnb, wao, wco, wo)


def kernel(x, p, rel_bias, norm_ffn1, w_ffn1_in, w_ffn1_out, norm_mix, w_in, q_norm, k_norm, sink, conv_w, conv_b, conv_ln_g, conv_ln_b, w_attn_out, w_conv_out, w_o, norm_ffn2, w_ffn2_in, w_ffn2_out, norm_pe, w_pe_gate, w_pe_proj):
    batch, seq, d = x.shape
    depth = p.shape[0]
    m = batch * seq
    assert seq % TILE_M == 0 and TILE_M % BLOCK == 0 and seq // BLOCK >= 2
    assert m % TILE_M_FFN == 0 and TILE_M_FFN % FFN_ROWS == 0

    kvc_lo, kvc_hi = Q_DIM, Q_DIM + 2 * KV_DIM + 2 * CONV_DIM
    w_qkvc = w_in[:, :, :kvc_hi].astype(BF16)
    w_g = w_in[:, :, kvc_hi:].astype(BF16)
    w1_in, w1_out = w_ffn1_in.astype(BF16), w_ffn1_out.astype(BF16)
    w2_in, w2_out = w_ffn2_in.astype(BF16), w_ffn2_out.astype(BF16)
    w_ao, w_co, w_oo = w_attn_out.astype(BF16), w_conv_out.astype(BF16), w_o.astype(BF16)
    w_pg, w_pp = w_pe_gate.astype(BF16), w_pe_proj.astype(BF16)

    bias = _bias_tables(rel_bias)
    ones = _blockdiag_ones(Q_DIM)
    vec = lambda a: a[:, None, :]
    q_gain = vec(jnp.tile(q_norm, (1, N_HEADS)) * (HEAD_DIM ** -0.5 * LOG2_E))
    k_gain = vec(jnp.tile(k_norm, (1, KV_HEADS)))
    sink_rows = jnp.repeat(sink.reshape(depth, KV_HEADS, 1, GROUP), BLOCK, axis=-1) * LOG2_E
    g_ffn1, g_mix, g_ffn2, g_pe = vec(norm_ffn1), vec(norm_mix), vec(norm_ffn2), vec(norm_pe)
    cb, lng, lnb = vec(conv_b), vec(conv_ln_g), vec(conv_ln_b)

    xf = x.reshape(m, d)
    pf = p.reshape(depth, m, p.shape[-1])
    for i in range(depth):
        h, q, k, v, c = _ffn_qkvc(i, xf, g_ffn1, w1_in, w1_out, g_mix, w_qkvc, q_gain, k_gain, ones)
        h = _mix(i, h, q, k, v, c, seq, g_mix, w_g, bias, sink_rows,
                 conv_w, cb, lng, lnb, w_ao, w_co, w_oo)
        xf = _ffn_ple(i, h, g_ffn2, w2_in, w2_out, g_pe, w_pg, pf, w_pp)
    return xf.reshape(batch, seq, d)
```

```python
import functools
import math

import numpy as np
import jax
import jax.numpy as jnp
from jax import lax
from jax.experimental import pallas as pl
from jax.experimental.pallas import tpu as pltpu

N_HEADS = 8
KV_HEADS = 2
GROUP = N_HEADS // KV_HEADS
HEAD_DIM = 64
Q_DIM = N_HEADS * HEAD_DIM
KV_DIM = KV_HEADS * HEAD_DIM
BLOCK = 128
NUM_BUCKETS = 32
MAX_DISTANCE = 128
CONV_DIM = 512
CONV_WIDTH = 31
CONV_PAD = CONV_WIDTH // 2
CONV_HALO = 16
NEG_INF = -1e9
LOG2_E = math.log2(math.e)

TILE_M = 512
TILE_M_FFN = 1024
FF_CHUNK = 256
FFN_ROWS = 256
CONV_ROWS = 64
CONV_TILES = 13
LANES = 128
SUBLANES = 8
BF16_ROWS = 16
VMEM_LIMIT_BYTES = 56 * 1024 * 1024

F32 = jnp.float32
BF16 = jnp.bfloat16


def _rms(x, g, eps=1e-6):
    return x * lax.rsqrt(jnp.mean(x * x, axis=-1, keepdims=True) + eps) * g


def _dot(a, b):
    return jnp.dot(a, b, preferred_element_type=F32)


def _group_sumsq(x, ones_blockdiag):
    x2 = x * x
    hi = x2.astype(BF16)
    lo = (x2 - hi.astype(F32)).astype(BF16)
    return _dot(hi, ones_blockdiag) + _dot(lo, ones_blockdiag)


def _swiglu_half_step(x_ref, g_ref, win_ref, wout_ref, act_ref):
    d_ff = wout_ref.shape[0]
    groups = [slice(r0, r0 + FFN_ROWS) for r0 in range(0, x_ref.shape[0], FFN_ROWS)]
    for rows in groups:
        xn = _rms(x_ref[rows, :], g_ref[...]).astype(BF16)
        for lo in range(0, d_ff, FF_CHUNK):
            gate = _dot(xn, win_ref[:, lo:lo + FF_CHUNK])
            up = _dot(xn, win_ref[:, d_ff + lo:d_ff + lo + FF_CHUNK])
            act_ref[rows, lo:lo + FF_CHUNK] = (gate * jax.nn.sigmoid(gate) * up).astype(BF16)
    return [(rows, x_ref[rows, :] + 0.5 * _dot(act_ref[rows, :], wout_ref[...])) for rows in groups]


def _ffn_kvc_kernel(x_ref, g1_ref, win_ref, wout_ref, gmix_ref, wkvc_ref, kg_ref, ones_ref,
                    h_ref, k_ref, v_ref, c_ref, act_ref):
    for rows, h in _swiglu_half_step(x_ref, g1_ref, win_ref, wout_ref, act_ref):
        h_ref[rows, :] = h
        u = _rms(h, gmix_ref[...]).astype(BF16)
        kvc = _dot(u, wkvc_ref[...])
        k = kvc[:, :KV_DIM]
        ss = _group_sumsq(k, ones_ref[...])
        k_ref[rows, :] = (k * lax.rsqrt(ss * (1.0 / HEAD_DIM) + 1e-6) * kg_ref[...]).astype(BF16)
        v_ref[:, rows] = kvc[:, KV_DIM:2 * KV_DIM].T.astype(BF16)
        c_val = kvc[:, 2 * KV_DIM:2 * KV_DIM + CONV_DIM]
        c_gate = kvc[:, 2 * KV_DIM + CONV_DIM:]
        c_ref[rows, :] = c_val * jax.nn.sigmoid(c_gate)


def _cast_on_the_side(src_refs, dst_refs):
    for src, dst in zip(src_refs, dst_refs):
        dst[...] = src[...].astype(BF16)


def _ffn_ple_kernel(n_cast, x_ref, g1_ref, win_ref, wout_ref, gpe_ref, wpg_ref, p_ref, wpp_ref, *refs):
    o_ref, act_ref = refs[n_cast], refs[-1]
    _cast_on_the_side(refs[:n_cast], refs[n_cast + 1:-1])
    for rows, h in _swiglu_half_step(x_ref, g1_ref, win_ref, wout_ref, act_ref):
        t = _rms(h, gpe_ref[...]).astype(BF16)
        gate = jax.nn.sigmoid(_dot(t, wpg_ref[...]))
        pe = _dot(p_ref[rows, :].astype(BF16), wpp_ref[...])
        o_ref[rows, :] = h + pe * gate


def _mix_kernel(tiles_per_seq,
                h_ref, kp_ref, kc_ref, kn_ref, vp_ref, vc_ref, vn_ref, cp_ref, cc_ref, cn_ref,
                gmix_ref, wq_ref, wg_ref, qgain_ref, ones_ref, bias_ref, sink_ref,
                cw_ref, cb_ref, lng_ref, lnb_ref, wao_ref, wco_ref, wo_ref, w2in_ref, w2out_ref,
                o_ref, w2in_bf_ref, w2out_bf_ref,
                kext_ref, vext_ref, cext_ref, conv_ref, y_ref, cact_ref, gate_ref):
    tm, d = h_ref.shape
    _cast_on_the_side((w2in_ref, w2out_ref), (w2in_bf_ref, w2out_bf_ref))
    n_blocks = tm // BLOCK
    tile = pl.program_id(0) % tiles_per_seq
    is_first = tile == 0
    is_last = tile == tiles_per_seq - 1

    h = h_ref[...]
    u = _rms(h, gmix_ref[...]).astype(BF16)
    q = _dot(u, wq_ref[...])
    ss = _group_sumsq(q, ones_ref[...])
    qn = (q * lax.rsqrt(ss * (1.0 / HEAD_DIM) + 1e-6) * qgain_ref[...]).astype(BF16)

    lane_groups = CONV_DIM // LANES
    seg = conv_ref.shape[1] // SUBLANES
    assert seg % 2 == 1 and seg * SUBLANES >= tm and seg % CONV_TILES == 0
    for g in range(lane_groups):
        lanes = slice(g * LANES, (g + 1) * LANES)
        cext_ref[g, :CONV_HALO] = jnp.where(is_first, 0.0, cp_ref[:, lanes])
        cext_ref[g, CONV_HALO:CONV_HALO + tm] = cc_ref[:, lanes]
        cext_ref[g, CONV_HALO + tm:] = jnp.where(is_last, 0.0, cn_ref[:, lanes])
    first_tap = CONV_HALO - CONV_PAD
    def conv_block(g, j0):
        lanes = slice(g * LANES, (g + 1) * LANES)
        windows = {}

        def window(j):
            if j not in windows:
                windows[j] = cext_ref[g, pl.ds(first_tap + j, SUBLANES, stride=seg), :]
            return windows[j]

        accs = [jnp.zeros((SUBLANES, LANES), F32)] * CONV_TILES
        for w in range(CONV_WIDTH):
            tap = cw_ref[w:w + 1, lanes]
            accs = [acc + window(j0 + t + w) * tap for t, acc in enumerate(accs)]
        for t, acc in enumerate(accs):
            conv_ref[g, pl.ds(j0 + t, SUBLANES, stride=seg), :] = acc + cb_ref[:, lanes]

    for g in range(lane_groups):
        for j0 in range(0, seg, CONV_TILES):
            conv_block(g, j0)
    for r0 in range(0, tm, CONV_ROWS):
        c = jnp.concatenate([conv_ref[g, r0:r0 + CONV_ROWS, :] for g in range(lane_groups)], axis=1)
        mu = jnp.mean(c, axis=-1, keepdims=True)
        var = jnp.mean(jnp.square(c - mu), axis=-1, keepdims=True)
        c = (c - mu) * lax.rsqrt(var + 1e-5) * lng_ref[...] + lnb_ref[...]
        cact_ref[r0:r0 + CONV_ROWS, :] = (c * jax.nn.sigmoid(c)).astype(BF16)
    y_conv = _dot(cact_ref[...], wco_ref[...])

    n_chunks = tm // CONV_ROWS
    gate_cols = gate_ref.shape[1] // n_chunks

    def gate_chunk(ci):
        g0 = ci * gate_cols
        gate_ref[:, g0:g0 + gate_cols] = jax.nn.sigmoid(_dot(u, wg_ref[:, g0:g0 + gate_cols]))

    kext_ref[:BLOCK] = kp_ref[...]
    kext_ref[BLOCK:BLOCK + tm] = kc_ref[...]
    kext_ref[BLOCK + tm:] = kn_ref[...]
    vext_ref[:, :BLOCK] = vp_ref[...]
    vext_ref[:, BLOCK:BLOCK + tm] = vc_ref[...]
    vext_ref[:, BLOCK + tm:] = vn_ref[...]

    def scores(r, kv):
        if r == 0:
            table = jnp.where(is_first, 0, 1)
        elif r == n_blocks - 1:
            table = jnp.where(is_last, 2, 1)
        else:
            table = 1
        rows = slice(r * BLOCK, (r + 1) * BLOCK)
        q4 = jnp.concatenate(
            [qn[rows, (kv * GROUP + j) * HEAD_DIM:(kv * GROUP + j + 1) * HEAD_DIM]
             for j in range(GROUP)], axis=0)
        kw = kext_ref[r * BLOCK:(r + 3) * BLOCK, kv * HEAD_DIM:(kv + 1) * HEAD_DIM]
        return lax.dot_general(kw, q4, (((1,), (1,)), ((), ())),
                               preferred_element_type=F32) + bias_ref[table, kv]

    def softmax(s, kv):
        m = jnp.max(s, axis=0, keepdims=True)
        e = jnp.exp2(s - m)
        denom = jnp.sum(e, axis=0, keepdims=True) + jnp.exp2(sink_ref[kv] - m)
        return e.astype(BF16), 1.0 / denom

    def weighted_values(e, inv_denom, r, kv):
        vwt = vext_ref[kv * HEAD_DIM:(kv + 1) * HEAD_DIM, r * BLOCK:(r + 3) * BLOCK]
        o = _dot(vwt, e) * inv_denom
        for pair in range(GROUP // 2):
            two = jnp.concatenate([o[:, (2 * pair) * BLOCK:(2 * pair + 1) * BLOCK],
                                   o[:, (2 * pair + 1) * BLOCK:(2 * pair + 2) * BLOCK]], axis=0)
            col = (kv * GROUP + 2 * pair) * HEAD_DIM
            y_ref[r * BLOCK:(r + 1) * BLOCK, col:col + 2 * HEAD_DIM] = two.T.astype(BF16)

    units = [(r, kv) for r in range(n_blocks) for kv in range(KV_HEADS)]
    s_of, p_of = {}, {}
    for t in range(len(units) + 2):
        if t < len(units):
            s_of[t] = scores(*units[t])
        if t < n_chunks:
            gate_chunk(t)
        if 0 <= t - 1 < len(units):
            p_of[t - 1] = softmax(s_of.pop(t - 1), units[t - 1][1])
        if 0 <= t - 2 < len(units):
            weighted_values(*p_of.pop(t - 2), *units[t - 2])
    y_attn = _dot(y_ref[...], wao_ref[...])

    merged = gate_ref[:, :d] * y_attn + gate_ref[:, d:] * y_conv
    o_ref[...] = h + _dot(merged.astype(BF16), wo_ref[...])


def _bias_tables(rel_bias):
    half = NUM_BUCKETS // 2
    max_exact = half // 2
    period = 4 * BLOCK
    k = np.arange(period)
    rel = np.where(k < 3 * BLOCK, k - BLOCK, k - BLOCK - period)
    n = np.abs(rel)
    nf = np.maximum(n, 1).astype(np.float64)
    large = max_exact + (np.log(nf / max_exact) / math.log(MAX_DISTANCE / max_exact)
                         * (half - max_exact)).astype(np.int32)
    buckets = np.where(rel > 0, half, 0) + np.where(n < max_exact, n, np.minimum(large, half - 1))
    vec = jnp.where((n <= BLOCK)[None, :], rel_bias.astype(F32)[buckets].T, NEG_INF)
    skew = jnp.tile(vec, (1, BLOCK))[:, :BLOCK * (period - 1)].reshape(N_HEADS, BLOCK, period - 1)
    bias = skew[:, :, :3 * BLOCK].reshape(KV_HEADS, GROUP * BLOCK, 3 * BLOCK)
    col = np.arange(3 * BLOCK)
    edge = np.stack([col >= BLOCK, col >= 0, col < 2 * BLOCK])
    tables = jnp.where(edge[:, None, None, :], bias[None], NEG_INF)
    return jnp.swapaxes(tables, -1, -2) * LOG2_E


def _blockdiag_ones(n):
    idx = np.arange(n) // HEAD_DIM
    return jnp.asarray(idx[:, None] == idx[None, :], dtype=BF16)


def _whole(arr):
    zeros = (0,) * arr.ndim
    return pl.BlockSpec(arr.shape, lambda i: zeros, pipeline_mode=pl.Buffered(1))


def _layer(arr, layer):
    index = (layer,) + (0,) * (arr.ndim - 1)
    return pl.BlockSpec((None,) + arr.shape[1:], lambda i: index, pipeline_mode=pl.Buffered(1))


def _rows(width, tm=TILE_M):
    return pl.BlockSpec((tm, width), lambda i: (i, 0))


def _cast_stream(w, layer, n_steps):
    rows, cols = w.shape[1:]
    chunks = next(c for c in range(n_steps, 0, -1)
                  if n_steps % c == 0 and rows % c == 0 and (rows // c) % BF16_ROWS == 0)
    per_chunk, block = n_steps // chunks, rows // chunks
    return (pl.BlockSpec((None, block, cols), lambda i: (layer, i // per_chunk, 0)),
            pl.BlockSpec((block, cols), lambda i: (i // per_chunk, 0)),
            jax.ShapeDtypeStruct((rows, cols), BF16))


_PARAMS = pltpu.CompilerParams(dimension_semantics=("arbitrary",),
                               vmem_limit_bytes=VMEM_LIMIT_BYTES)


def _ffn_kvc(layer, x, g1, win, wout, gmix, wkvc, kg, ones):
    m, d = x.shape
    d_ff = wout.shape[0]
    tm = TILE_M_FFN
    lay = lambda a: _layer(a, layer)
    return pl.pallas_call(
        _ffn_kvc_kernel,
        grid=(m // tm,),
        in_specs=[_rows(d, tm), lay(g1), _whole(win), _whole(wout), lay(gmix), lay(wkvc), lay(kg),
                  _whole(ones)],
        out_specs=[_rows(d, tm), _rows(KV_DIM, tm), pl.BlockSpec((KV_DIM, tm), lambda i: (0, i)),
                   _rows(CONV_DIM, tm)],
        out_shape=[jax.ShapeDtypeStruct((m, d), F32), jax.ShapeDtypeStruct((m, KV_DIM), BF16),
                   jax.ShapeDtypeStruct((KV_DIM, m), BF16), jax.ShapeDtypeStruct((m, CONV_DIM), F32)],
        scratch_shapes=[pltpu.VMEM((tm, d_ff), BF16)],
        compiler_params=_PARAMS,
        name="ffn_kvc",
    )(x, g1, win, wout, gmix, wkvc, kg, ones)


def _ffn_ple(layer, x, g1, win, wout, gpe, wpg, p, wpp, cast_next=()):
    m, d = x.shape
    d_ff = wout.shape[0]
    tm = TILE_M_FFN
    lay = lambda a: _layer(a, layer)
    p_spec = pl.BlockSpec((None, tm, p.shape[2]), lambda i: (layer, i, 0))
    casts = [_cast_stream(w, layer + 1, m // tm) for w in cast_next]
    return pl.pallas_call(
        functools.partial(_ffn_ple_kernel, len(casts)),
        grid=(m // tm,),
        in_specs=[_rows(d, tm), lay(g1), _whole(win), _whole(wout), lay(gpe), lay(wpg), p_spec,
                  lay(wpp)] + [c[0] for c in casts],
        out_specs=[_rows(d, tm)] + [c[1] for c in casts],
        out_shape=[jax.ShapeDtypeStruct((m, d), F32)] + [c[2] for c in casts],
        scratch_shapes=[pltpu.VMEM((tm, d_ff), BF16)],
        compiler_params=_PARAMS,
        name="ffn_ple",
    )(x, g1, win, wout, gpe, wpg, p, wpp, *cast_next)


def _mix(layer, h, k, v, c, seq, gmix, wq, wg, qgain, ones, bias, sink, cw, cb, lng, lnb,
         wao, wco, wo, w2in, w2out):
    m, d = h.shape
    tm = TILE_M
    tiles_per_seq = seq // tm
    kv_blocks = tm // BLOCK
    last_kv = m // BLOCK - 1
    seg = tm // SUBLANES + 1
    reach = SUBLANES * seg - tm + CONV_PAD
    next_halo = next(n for n in (16, 32, 64, 128) if n >= reach)
    assert tm % next_halo == 0 and tm % CONV_HALO == 0

    def halo(rows, width, stride, last, side):
        if side < 0:
            return pl.BlockSpec((rows, width), lambda i: (jnp.maximum(i * stride - 1, 0), 0))
        return pl.BlockSpec((rows, width), lambda i: (jnp.minimum((i + 1) * stride, last), 0))

    k_specs = [halo(BLOCK, KV_DIM, kv_blocks, last_kv, -1), _rows(KV_DIM),
               halo(BLOCK, KV_DIM, kv_blocks, last_kv, +1)]
    v_specs = [pl.BlockSpec((KV_DIM, BLOCK), lambda i: (0, jnp.maximum(i * kv_blocks - 1, 0))),
               pl.BlockSpec((KV_DIM, tm), lambda i: (0, i)),
               pl.BlockSpec((KV_DIM, BLOCK), lambda i: (0, jnp.minimum((i + 1) * kv_blocks, last_kv)))]
    c_specs = [halo(CONV_HALO, CONV_DIM, tm // CONV_HALO, m // CONV_HALO - 1, -1), _rows(CONV_DIM),
               halo(next_halo, CONV_DIM, tm // next_halo, m // next_halo - 1, +1)]
    lay = lambda a: _layer(a, layer)
    casts = [_cast_stream(w, layer, m // tm) for w in (w2in, w2out)]
    return pl.pallas_call(
        functools.partial(_mix_kernel, tiles_per_seq),
        grid=(m // tm,),
        in_specs=[_rows(d)] + k_specs + v_specs + c_specs + [
            lay(gmix), lay(wq), lay(wg), lay(qgain), _whole(ones), _whole(bias), lay(sink),
            lay(cw), lay(cb), lay(lng), lay(lnb), lay(wao), lay(wco), lay(wo)] + [c[0] for c in casts],
        out_specs=[_rows(d)] + [c[1] for c in casts],
        out_shape=[jax.ShapeDtypeStruct((m, d), F32)] + [c[2] for c in casts],
        scratch_shapes=[pltpu.VMEM((tm + 2 * BLOCK, KV_DIM), BF16),
                        pltpu.VMEM((KV_DIM, tm + 2 * BLOCK), BF16),
                        pltpu.VMEM((CONV_DIM // LANES, CONV_HALO + tm + next_halo, LANES), F32),
                        pltpu.VMEM((CONV_DIM // LANES, SUBLANES * seg, LANES), F32),
                        pltpu.VMEM((tm, Q_DIM), BF16),
                        pltpu.VMEM((tm, CONV_DIM), BF16),
                        pltpu.VMEM((tm, 2 * d), F32)],
        compiler_params=_PARAMS,
        name="mix",
    )(h, k, k, k, v, v, v, c, c, c, gmix, wq, wg, qgain, ones, bias, sink, cw, cb, lng, lnb,
      wao, wco, wo, w2in, w2out)


def kernel(x, p, rel_bias, norm_ffn1, w_ffn1_in, w_ffn1_out, norm_mix, w_in, q_norm, k_norm, sink, conv_w, conv_b, conv_ln_g, conv_ln_b, w_attn_out, w_conv_out, w_o, norm_ffn2, w_ffn2_in, w_ffn2_out, norm_pe, w_pe_gate, w_pe_proj):
    batch, seq, d = x.shape
    depth = p.shape[0]
    m = batch * seq
    assert seq % TILE_M == 0 and TILE_M % BLOCK == 0 and seq // BLOCK >= 2
    assert m % TILE_M_FFN == 0 and TILE_M_FFN % FFN_ROWS == 0

    kvc_lo, kvc_hi = Q_DIM, Q_DIM + 2 * KV_DIM + 2 * CONV_DIM
    w_q = w_in[:, :, :kvc_lo].astype(BF16)
    w_kvc = w_in[:, :, kvc_lo:kvc_hi].astype(BF16)
    w_g = w_in[:, :, kvc_hi:].astype(BF16)
    w1 = [w_ffn1_in[0].astype(BF16), w_ffn1_out[0].astype(BF16)]
    w_ao, w_co, w_oo = w_attn_out.astype(BF16), w_conv_out.astype(BF16), w_o.astype(BF16)
    w_pg, w_pp = w_pe_gate.astype(BF16), w_pe_proj.astype(BF16)

    bias = _bias_tables(rel_bias)
    ones_q, ones_k = _blockdiag_ones(Q_DIM), _blockdiag_ones(KV_DIM)
    vec = lambda a: a[:, None, :]
    q_gain = vec(jnp.tile(q_norm, (1, N_HEADS)) * (HEAD_DIM ** -0.5 * LOG2_E))
    k_gain = vec(jnp.tile(k_norm, (1, KV_HEADS)))
    sink_rows = jnp.repeat(sink.reshape(depth, KV_HEADS, 1, GROUP), BLOCK, axis=-1) * LOG2_E
    g_ffn1, g_mix, g_ffn2, g_pe = vec(norm_ffn1), vec(norm_mix), vec(norm_ffn2), vec(norm_pe)
    cb, lng, lnb = vec(conv_b), vec(conv_ln_g), vec(conv_ln_b)

    xf = x.reshape(m, d)
    pf = p.reshape(depth, m, p.shape[-1])
    for i in range(depth):
        h, k, v, c = _ffn_kvc(i, xf, g_ffn1, w1[0], w1[1], g_mix, w_kvc, k_gain, ones_k)
        h, w2_in, w2_out = _mix(i, h, k, v, c, seq, g_mix, w_q, w_g, q_gain, ones_q, bias, sink_rows,
                                conv_w, cb, lng, lnb, w_ao, w_co, w_oo, w_ffn2_in, w_ffn2_out)
        cast_next = (w_ffn1_in, w_ffn1_out) if i + 1 < depth else ()
        xf, *w1 = _ffn_ple(i, h, g_ffn2, w2_in, w2_out, g_pe, w_pg, pf, w_pp, cast_next)
    return xf.reshape(batch, seq, d)
```

```python
import functools
import math

import numpy as np
import jax
import jax.numpy as jnp
from jax import lax
from jax.experimental import pallas as pl
from jax.experimental.pallas import tpu as pltpu

N_HEADS = 8
KV_HEADS = 2
GROUP = N_HEADS // KV_HEADS
HEAD_DIM = 64
Q_DIM = N_HEADS * HEAD_DIM
KV_DIM = KV_HEADS * HEAD_DIM
BLOCK = 128
NUM_BUCKETS = 32
MAX_DISTANCE = 128
CONV_DIM = 512
CONV_WIDTH = 31
CONV_PAD = CONV_WIDTH // 2
CONV_HALO = 16
QKVC_COLS = Q_DIM + 2 * KV_DIM + 2 * CONV_DIM
NEG_INF = -1e9
LOG2_E = math.log2(math.e)

TILE_M = 512
TILE_M_FFN = 1024
FF_CHUNK = 256
FFN_ROWS = 256
CONV_ROWS = 64
CONV_TILES = 13
LANES = 128
SUBLANES = 8
BF16_ROWS = 16
VMEM_LIMIT_BYTES = 56 * 1024 * 1024

F32 = jnp.float32
BF16 = jnp.bfloat16


def _rms(x, g, eps=1e-6):
    return x * lax.rsqrt(jnp.mean(x * x, axis=-1, keepdims=True) + eps) * g


def _dot(a, b):
    return jnp.dot(a, b, preferred_element_type=F32)


def _group_sumsq(x, ones_blockdiag):
    x2 = x * x
    hi = x2.astype(BF16)
    lo = (x2 - hi.astype(F32)).astype(BF16)
    return _dot(hi, ones_blockdiag) + _dot(lo, ones_blockdiag)


def _swiglu_half_step(x_ref, g_ref, win_ref, wout_ref, act_ref):
    d_ff = wout_ref.shape[0]
    groups = [slice(r0, r0 + FFN_ROWS) for r0 in range(0, x_ref.shape[0], FFN_ROWS)]
    for rows in groups:
        xn = _rms(x_ref[rows, :], g_ref[...]).astype(BF16)
        for lo in range(0, d_ff, FF_CHUNK):
            gate = _dot(xn, win_ref[:, lo:lo + FF_CHUNK])
            up = _dot(xn, win_ref[:, d_ff + lo:d_ff + lo + FF_CHUNK])
            act_ref[rows, lo:lo + FF_CHUNK] = (gate * jax.nn.sigmoid(gate) * up).astype(BF16)
    return [(rows, x_ref[rows, :] + 0.5 * _dot(act_ref[rows, :], wout_ref[...])) for rows in groups]


def _ffn_kvc_kernel(x_ref, g1_ref, win_ref, wout_ref, gmix_ref, wqkvc_ref, kg_ref, ones_ref,
                    h_ref, k_ref, v_ref, c_ref, act_ref):
    for rows, h in _swiglu_half_step(x_ref, g1_ref, win_ref, wout_ref, act_ref):
        h_ref[rows, :] = h
        u = _rms(h, gmix_ref[...]).astype(BF16)
        kvc = _dot(u, wqkvc_ref[:, Q_DIM:])
        k = kvc[:, :KV_DIM]
        ss = _group_sumsq(k, ones_ref[...])
        k_ref[rows, :] = (k * lax.rsqrt(ss * (1.0 / HEAD_DIM) + 1e-6) * kg_ref[...]).astype(BF16)
        v_ref[:, rows] = kvc[:, KV_DIM:2 * KV_DIM].T.astype(BF16)
        c_val = kvc[:, 2 * KV_DIM:2 * KV_DIM + CONV_DIM]
        c_gate = kvc[:, 2 * KV_DIM + CONV_DIM:]
        c_ref[rows, :] = c_val * jax.nn.sigmoid(c_gate)


def _cast_on_the_side(src_refs, dst_refs):
    for src, dst in zip(src_refs, dst_refs):
        dst[...] = src[...].astype(BF16)


def _ffn_ple_kernel(n_cast, x_ref, g1_ref, win_ref, wout_ref, gpe_ref, wpg_ref, p_ref, wpp_ref, *refs):
    o_ref, act_ref = refs[n_cast], refs[-1]
    _cast_on_the_side(refs[:n_cast], refs[n_cast + 1:-1])
    for rows, h in _swiglu_half_step(x_ref, g1_ref, win_ref, wout_ref, act_ref):
        t = _rms(h, gpe_ref[...]).astype(BF16)
        gate = jax.nn.sigmoid(_dot(t, wpg_ref[...]))
        pe = _dot(p_ref[rows, :].astype(BF16), wpp_ref[...])
        o_ref[rows, :] = h + pe * gate


def _mix_kernel(tiles_per_seq, n_cast,
                h_ref, kp_ref, kc_ref, kn_ref, vp_ref, vc_ref, vn_ref, cp_ref, cc_ref, cn_ref,
                gmix_ref, win_ref, qgain_ref, ones_ref, bias_ref, sink_ref,
                cw_ref, cb_ref, lng_ref, lnb_ref, wao_ref, wco_ref, wo_ref, *refs):
    o_ref = refs[n_cast]
    kext_ref, vext_ref, cext_ref, conv_ref, y_ref, cact_ref, gate_ref = refs[2 * n_cast + 1:]
    _cast_on_the_side(refs[:n_cast], refs[n_cast + 1:2 * n_cast + 1])
    tm, d = h_ref.shape
    gate_col0 = win_ref.shape[1] - gate_ref.shape[1]
    n_blocks = tm // BLOCK
    tile = pl.program_id(0) % tiles_per_seq
    is_first = tile == 0
    is_last = tile == tiles_per_seq - 1

    h = h_ref[...]
    u = _rms(h, gmix_ref[...]).astype(BF16)
    q = _dot(u, win_ref[:, :Q_DIM])
    ss = _group_sumsq(q, ones_ref[...])
    qn = (q * lax.rsqrt(ss * (1.0 / HEAD_DIM) + 1e-6) * qgain_ref[...]).astype(BF16)

    lane_groups = CONV_DIM // LANES
    seg = conv_ref.shape[1] // SUBLANES
    assert seg % 2 == 1 and seg * SUBLANES >= tm and seg % CONV_TILES == 0
    for g in range(lane_groups):
        lanes = slice(g * LANES, (g + 1) * LANES)
        cext_ref[g, :CONV_HALO] = jnp.where(is_first, 0.0, cp_ref[:, lanes])
        cext_ref[g, CONV_HALO:CONV_HALO + tm] = cc_ref[:, lanes]
        cext_ref[g, CONV_HALO + tm:] = jnp.where(is_last, 0.0, cn_ref[:, lanes])
    first_tap = CONV_HALO - CONV_PAD
    def conv_block(g, j0):
        lanes = slice(g * LANES, (g + 1) * LANES)
        windows = {}

        def window(j):
            if j not in windows:
                windows[j] = cext_ref[g, pl.ds(first_tap + j, SUBLANES, stride=seg), :]
            return windows[j]

        accs = [jnp.zeros((SUBLANES, LANES), F32)] * CONV_TILES
        for w in range(CONV_WIDTH):
            tap = cw_ref[w:w + 1, lanes]
            accs = [acc + window(j0 + t + w) * tap for t, acc in enumerate(accs)]
        for t, acc in enumerate(accs):
            conv_ref[g, pl.ds(j0 + t, SUBLANES, stride=seg), :] = acc + cb_ref[:, lanes]

    for g in range(lane_groups):
        for j0 in range(0, seg, CONV_TILES):
            conv_block(g, j0)
    for r0 in range(0, tm, CONV_ROWS):
        c = jnp.concatenate([conv_ref[g, r0:r0 + CONV_ROWS, :] for g in range(lane_groups)], axis=1)
        mu = jnp.mean(c, axis=-1, keepdims=True)
        var = jnp.mean(jnp.square(c - mu), axis=-1, keepdims=True)
        c = (c - mu) * lax.rsqrt(var + 1e-5) * lng_ref[...] + lnb_ref[...]
        cact_ref[r0:r0 + CONV_ROWS, :] = (c * jax.nn.sigmoid(c)).astype(BF16)
    y_conv = _dot(cact_ref[...], wco_ref[...])

    n_chunks = tm // CONV_ROWS
    gate_cols = gate_ref.shape[1] // n_chunks

    def gate_chunk(ci):
        g0 = ci * gate_cols
        gate_ref[:, g0:g0 + gate_cols] = jax.nn.sigmoid(
            _dot(u, win_ref[:, gate_col0 + g0:gate_col0 + g0 + gate_cols]))

    kext_ref[:BLOCK] = kp_ref[...]
    kext_ref[BLOCK:BLOCK + tm] = kc_ref[...]
    kext_ref[BLOCK + tm:] = kn_ref[...]
    vext_ref[:, :BLOCK] = vp_ref[...]
    vext_ref[:, BLOCK:BLOCK + tm] = vc_ref[...]
    vext_ref[:, BLOCK + tm:] = vn_ref[...]

    def scores(r, kv):
        if r == 0:
            table = jnp.where(is_first, 0, 1)
        elif r == n_blocks - 1:
            table = jnp.where(is_last, 2, 1)
        else:
            table = 1
        rows = slice(r * BLOCK, (r + 1) * BLOCK)
        q4 = jnp.concatenate(
            [qn[rows, (kv * GROUP + j) * HEAD_DIM:(kv * GROUP + j + 1) * HEAD_DIM]
             for j in range(GROUP)], axis=0)
        kw = kext_ref[r * BLOCK:(r + 3) * BLOCK, kv * HEAD_DIM:(kv + 1) * HEAD_DIM]
        return lax.dot_general(kw, q4, (((1,), (1,)), ((), ())),
                               preferred_element_type=F32) + bias_ref[table, kv]

    def softmax(s, kv):
        m = jnp.max(s, axis=0, keepdims=True)
        e = jnp.exp2(s - m)
        denom = jnp.sum(e, axis=0, keepdims=True) + jnp.exp2(sink_ref[kv] - m)
        return e.astype(BF16), 1.0 / denom

    def weighted_values(e, inv_denom, r, kv):
        vwt = vext_ref[kv * HEAD_DIM:(kv + 1) * HEAD_DIM, r * BLOCK:(r + 3) * BLOCK]
        o = _dot(vwt, e) * inv_denom
        for pair in range(GROUP // 2):
            two = jnp.concatenate([o[:, (2 * pair) * BLOCK:(2 * pair + 1) * BLOCK],
                                   o[:, (2 * pair + 1) * BLOCK:(2 * pair + 2) * BLOCK]], axis=0)
            col = (kv * GROUP + 2 * pair) * HEAD_DIM
            y_ref[r * BLOCK:(r + 1) * BLOCK, col:col + 2 * HEAD_DIM] = two.T.astype(BF16)

    units = [(r, kv) for r in range(n_blocks) for kv in range(KV_HEADS)]
    s_of, p_of = {}, {}
    for t in range(len(units) + 2):
        if t < len(units):
            s_of[t] = scores(*units[t])
        if t < n_chunks:
            gate_chunk(t)
        if 0 <= t - 1 < len(units):
            p_of[t - 1] = softmax(s_of.pop(t - 1), units[t - 1][1])
        if 0 <= t - 2 < len(units):
            weighted_values(*p_of.pop(t - 2), *units[t - 2])
    y_attn = _dot(y_ref[...], wao_ref[...])

    merged = gate_ref[:, :d] * y_attn + gate_ref[:, d:] * y_conv
    o_ref[...] = h + _dot(merged.astype(BF16), wo_ref[...])


def _bias_tables(rel_bias):
    half = NUM_BUCKETS // 2
    max_exact = half // 2
    period = 4 * BLOCK
    k = np.arange(period)
    rel = np.where(k < 3 * BLOCK, k - BLOCK, k - BLOCK - period)
    n = np.abs(rel)
    nf = np.maximum(n, 1).astype(np.float64)
    large = max_exact + (np.log(nf / max_exact) / math.log(MAX_DISTANCE / max_exact)
                         * (half - max_exact)).astype(np.int32)
    buckets = np.where(rel > 0, half, 0) + np.where(n < max_exact, n, np.minimum(large, half - 1))
    vec = jnp.where((n <= BLOCK)[None, :], rel_bias.astype(F32)[buckets].T, NEG_INF)
    skew = jnp.tile(vec, (1, BLOCK))[:, :BLOCK * (period - 1)].reshape(N_HEADS, BLOCK, period - 1)
    bias = skew[:, :, :3 * BLOCK].reshape(KV_HEADS, GROUP * BLOCK, 3 * BLOCK)
    col = np.arange(3 * BLOCK)
    edge = np.stack([col >= BLOCK, col >= 0, col < 2 * BLOCK])
    tables = jnp.where(edge[:, None, None, :], bias[None], NEG_INF)
    return jnp.swapaxes(tables, -1, -2) * LOG2_E


def _blockdiag_ones(n):
    idx = np.arange(n) // HEAD_DIM
    return jnp.asarray(idx[:, None] == idx[None, :], dtype=BF16)


def _whole(arr):
    zeros = (0,) * arr.ndim
    return pl.BlockSpec(arr.shape, lambda i: zeros, pipeline_mode=pl.Buffered(1))


def _layer(arr, layer):
    index = (layer,) + (0,) * (arr.ndim - 1)
    return pl.BlockSpec((None,) + arr.shape[1:], lambda i: index, pipeline_mode=pl.Buffered(1))


def _rows(width, tm=TILE_M):
    return pl.BlockSpec((tm, width), lambda i: (i, 0))


def _cast_stream(w, layer, n_steps):
    rows, cols = w.shape[1:]
    chunks = next(c for c in range(n_steps, 0, -1)
                  if n_steps % c == 0 and rows % c == 0 and (rows // c) % BF16_ROWS == 0)
    per_chunk, block = n_steps // chunks, rows // chunks
    return (pl.BlockSpec((None, block, cols), lambda i: (layer, i // per_chunk, 0)),
            pl.BlockSpec((block, cols), lambda i: (i // per_chunk, 0)),
            jax.ShapeDtypeStruct((rows, cols), BF16))


_PARAMS = pltpu.CompilerParams(dimension_semantics=("arbitrary",),
                               vmem_limit_bytes=VMEM_LIMIT_BYTES)


def _ffn_kvc(layer, x, g1, win, wout, gmix, wmix, kg, ones):
    m, d = x.shape
    d_ff = wout.shape[0]
    tm = TILE_M_FFN
    lay = lambda a: _layer(a, layer)
    return pl.pallas_call(
        _ffn_kvc_kernel,
        grid=(m // tm,),
        in_specs=[_rows(d, tm), lay(g1), _whole(win), _whole(wout), lay(gmix),
                  pl.BlockSpec((d, QKVC_COLS), lambda i: (0, 0), pipeline_mode=pl.Buffered(1)),
                  lay(kg), _whole(ones)],
        out_specs=[_rows(d, tm), _rows(KV_DIM, tm), pl.BlockSpec((KV_DIM, tm), lambda i: (0, i)),
                   _rows(CONV_DIM, tm)],
        out_shape=[jax.ShapeDtypeStruct((m, d), F32), jax.ShapeDtypeStruct((m, KV_DIM), BF16),
                   jax.ShapeDtypeStruct((KV_DIM, m), BF16), jax.ShapeDtypeStruct((m, CONV_DIM), F32)],
        scratch_shapes=[pltpu.VMEM((tm, d_ff), BF16)],
        compiler_params=_PARAMS,
        name="ffn_kvc",
    )(x, g1, win, wout, gmix, wmix, kg, ones)


def _ffn_ple(layer, x, g1, win, wout, gpe, wpg, p, wpp, cast_next=()):
    m, d = x.shape
    d_ff = wout.shape[0]
    tm = TILE_M_FFN
    lay = lambda a: _layer(a, layer)
    p_spec = pl.BlockSpec((None, tm, p.shape[2]), lambda i: (layer, i, 0))
    casts = [_cast_stream(w, layer + 1, m // tm) for w in cast_next]
    return pl.pallas_call(
        functools.partial(_ffn_ple_kernel, len(casts)),
        grid=(m // tm,),
        in_specs=[_rows(d, tm), lay(g1), _whole(win), _whole(wout), lay(gpe), _whole(wpg), p_spec,
                  _whole(wpp)] + [c[0] for c in casts],
        out_specs=[_rows(d, tm)] + [c[1] for c in casts],
        out_shape=[jax.ShapeDtypeStruct((m, d), F32)] + [c[2] for c in casts],
        scratch_shapes=[pltpu.VMEM((tm, d_ff), BF16)],
        compiler_params=_PARAMS,
        name="ffn_ple",
    )(x, g1, win, wout, gpe, wpg, p, wpp, *cast_next)


def _mix(layer, h, k, v, c, seq, gmix, wmix, qgain, ones, bias, sink, cw, cb, lng, lnb,
         wao, wco, wo, cast_here, cast_next):
    m, d = h.shape
    tm = TILE_M
    tiles_per_seq = seq // tm
    kv_blocks = tm // BLOCK
    last_kv = m // BLOCK - 1
    seg = tm // SUBLANES + 1
    reach = SUBLANES * seg - tm + CONV_PAD
    next_halo = next(n for n in (16, 32, 64, 128) if n >= reach)
    assert tm % next_halo == 0 and tm % CONV_HALO == 0

    def halo(rows, width, stride, last, side):
        if side < 0:
            return pl.BlockSpec((rows, width), lambda i: (jnp.maximum(i * stride - 1, 0), 0))
        return pl.BlockSpec((rows, width), lambda i: (jnp.minimum((i + 1) * stride, last), 0))

    k_specs = [halo(BLOCK, KV_DIM, kv_blocks, last_kv, -1), _rows(KV_DIM),
               halo(BLOCK, KV_DIM, kv_blocks, last_kv, +1)]
    v_specs = [pl.BlockSpec((KV_DIM, BLOCK), lambda i: (0, jnp.maximum(i * kv_blocks - 1, 0))),
               pl.BlockSpec((KV_DIM, tm), lambda i: (0, i)),
               pl.BlockSpec((KV_DIM, BLOCK), lambda i: (0, jnp.minimum((i + 1) * kv_blocks, last_kv)))]
    c_specs = [halo(CONV_HALO, CONV_DIM, tm // CONV_HALO, m // CONV_HALO - 1, -1), _rows(CONV_DIM),
               halo(next_halo, CONV_DIM, tm // next_halo, m // next_halo - 1, +1)]
    lay = lambda a: _layer(a, layer)
    casts = ([_cast_stream(w, layer, m // tm) for w in cast_here]
             + [_cast_stream(w, layer + 1, m // tm) for w in cast_next])
    return pl.pallas_call(
        functools.partial(_mix_kernel, tiles_per_seq, len(casts)),
        grid=(m // tm,),
        in_specs=[_rows(d)] + k_specs + v_specs + c_specs + [
            lay(gmix), _whole(wmix), lay(qgain), _whole(ones), _whole(bias), lay(sink),
            lay(cw), lay(cb), lay(lng), lay(lnb), _whole(wao), _whole(wco), _whole(wo)]
                 + [c[0] for c in casts],
        out_specs=[_rows(d)] + [c[1] for c in casts],
        out_shape=[jax.ShapeDtypeStruct((m, d), F32)] + [c[2] for c in casts],
        scratch_shapes=[pltpu.VMEM((tm + 2 * BLOCK, KV_DIM), BF16),
                        pltpu.VMEM((KV_DIM, tm + 2 * BLOCK), BF16),
                        pltpu.VMEM((CONV_DIM // LANES, CONV_HALO + tm + next_halo, LANES), F32),
                        pltpu.VMEM((CONV_DIM // LANES, SUBLANES * seg, LANES), F32),
                        pltpu.VMEM((tm, Q_DIM), BF16),
                        pltpu.VMEM((tm, CONV_DIM), BF16),
                        pltpu.VMEM((tm, 2 * d), F32)],
        compiler_params=_PARAMS,
        name="mix",
    )(h, k, k, k, v, v, v, c, c, c, gmix, wmix, qgain, ones, bias, sink, cw, cb, lng, lnb,
      wao, wco, wo, *cast_here, *cast_next)


def kernel(x, p, rel_bias, norm_ffn1, w_ffn1_in, w_ffn1_out, norm_mix, w_in, q_norm, k_norm, sink, conv_w, conv_b, conv_ln_g, conv_ln_b, w_attn_out, w_conv_out, w_o, norm_ffn2, w_ffn2_in, w_ffn2_out, norm_pe, w_pe_gate, w_pe_proj):
    batch, seq, d = x.shape
    depth = p.shape[0]
    m = batch * seq
    assert seq % TILE_M == 0 and TILE_M % BLOCK == 0 and seq // BLOCK >= 2
    assert m % TILE_M_FFN == 0 and TILE_M_FFN % FFN_ROWS == 0

    mix_weights = (w_in, w_attn_out, w_conv_out, w_o, w_pe_gate, w_pe_proj)
    w1 = [w_ffn1_in[0].astype(BF16), w_ffn1_out[0].astype(BF16)]
    w_mix, w_ao, w_co, w_oo, w_pg, w_pp = [w[0].astype(BF16) for w in mix_weights]

    bias = _bias_tables(rel_bias)
    ones_q, ones_k = _blockdiag_ones(Q_DIM), _blockdiag_ones(KV_DIM)
    vec = lambda a: a[:, None, :]
    q_gain = vec(jnp.tile(q_norm, (1, N_HEADS)) * (HEAD_DIM ** -0.5 * LOG2_E))
    k_gain = vec(jnp.tile(k_norm, (1, KV_HEADS)))
    sink_rows = jnp.repeat(sink.reshape(depth, KV_HEADS, 1, GROUP), BLOCK, axis=-1) * LOG2_E
    g_ffn1, g_mix, g_ffn2, g_pe = vec(norm_ffn1), vec(norm_mix), vec(norm_ffn2), vec(norm_pe)
    cb, lng, lnb = vec(conv_b), vec(conv_ln_g), vec(conv_ln_b)

    xf = x.reshape(m, d)
    pf = p.reshape(depth, m, p.shape[-1])
    for i in range(depth):
        last = i + 1 == depth
        h, k, v, c = _ffn_kvc(i, xf, g_ffn1, w1[0], w1[1], g_mix, w_mix, k_gain, ones_k)
        h, w2_in, w2_out, *next_mix = _mix(
            i, h, k, v, c, seq, g_mix, w_mix, q_gain, ones_q, bias, sink_rows, conv_w, cb, lng, lnb,
            w_ao, w_co, w_oo, (w_ffn2_in, w_ffn2_out), () if last else mix_weights)
        xf, *w1 = _ffn_ple(i, h, g_ffn2, w2_in, w2_out, g_pe, w_pg, pf, w_pp,
                           () if last else (w_ffn1_in, w_ffn1_out))
        if not last:
            w_mix, w_ao, w_co, w_oo, w_pg, w_pp = next_mix
    return xf.reshape(batch, seq, d)
```

```python
import functools
import math

import numpy as np
import jax
import jax.numpy as jnp
from jax import lax
from jax.experimental import pallas as pl
from jax.experimental.pallas import tpu as pltpu

N_HEADS = 8
KV_HEADS = 2
GROUP = N_HEADS // KV_HEADS
HEAD_DIM = 64
Q_DIM = N_HEADS * HEAD_DIM
KV_DIM = KV_HEADS * HEAD_DIM
BLOCK = 128
NUM_BUCKETS = 32
MAX_DISTANCE = 128
CONV_DIM = 512
CONV_WIDTH = 31
CONV_PAD = CONV_WIDTH // 2
CONV_HALO = 16
QKVC_COLS = Q_DIM + 2 * KV_DIM + 2 * CONV_DIM
NEG_INF = -1e9
LOG2_E = math.log2(math.e)

TILE_M = 512
TILE_M_FFN = 1024
FF_CHUNK = 256
FFN_ROWS = 256
CONV_ROWS = 64
GATE_CHUNK = 256
CONV_TILES = 13
LANES = 128
SUBLANES = 8
BF16_ROWS = 16
VMEM_LIMIT_BYTES = 56 * 1024 * 1024

F32 = jnp.float32
BF16 = jnp.bfloat16


def _rms(x, g, eps=1e-6):
    return x * lax.rsqrt(jnp.mean(x * x, axis=-1, keepdims=True) + eps) * g


def _dot(a, b):
    return jnp.dot(a, b, preferred_element_type=F32)


def _group_sumsq(x, ones_blockdiag):
    x2 = x * x
    hi = x2.astype(BF16)
    lo = (x2 - hi.astype(F32)).astype(BF16)
    return _dot(hi, ones_blockdiag) + _dot(lo, ones_blockdiag)


def _swiglu_half_step(x_ref, g_ref, win_ref, wout_ref, act_ref):
    d_ff = wout_ref.shape[0]
    groups = [slice(r0, r0 + FFN_ROWS) for r0 in range(0, x_ref.shape[0], FFN_ROWS)]
    for rows in groups:
        xn = _rms(x_ref[rows, :], g_ref[...]).astype(BF16)
        for lo in range(0, d_ff, FF_CHUNK):
            gate = _dot(xn, win_ref[:, lo:lo + FF_CHUNK])
            up = _dot(xn, win_ref[:, d_ff + lo:d_ff + lo + FF_CHUNK])
            act_ref[rows, lo:lo + FF_CHUNK] = (gate * jax.nn.sigmoid(gate) * up).astype(BF16)
    return [(rows, x_ref[rows, :] + 0.5 * _dot(act_ref[rows, :], wout_ref[...])) for rows in groups]


def _ffn_kvc_kernel(x_ref, g1_ref, win_ref, wout_ref, gmix_ref, wqkvc_ref, kg_ref, ones_ref,
                    h_ref, k_ref, v_ref, c_ref, act_ref):
    for rows, h in _swiglu_half_step(x_ref, g1_ref, win_ref, wout_ref, act_ref):
        h_ref[rows, :] = h
        u = _rms(h, gmix_ref[...]).astype(BF16)
        kvc = _dot(u, wqkvc_ref[:, Q_DIM:])
        k = kvc[:, :KV_DIM]
        ss = _group_sumsq(k, ones_ref[...])
        k_ref[rows, :] = (k * lax.rsqrt(ss * (1.0 / HEAD_DIM) + 1e-6) * kg_ref[...]).astype(BF16)
        v_ref[:, rows] = kvc[:, KV_DIM:2 * KV_DIM].T.astype(BF16)
        c_val = kvc[:, 2 * KV_DIM:2 * KV_DIM + CONV_DIM]
        c_gate = kvc[:, 2 * KV_DIM + CONV_DIM:]
        c_ref[rows, :] = c_val * jax.nn.sigmoid(c_gate)


def _cast_on_the_side(src_refs, dst_refs):
    for src, dst in zip(src_refs, dst_refs):
        dst[...] = src[...].astype(BF16)


def _ffn_ple_kernel(n_cast, x_ref, g1_ref, win_ref, wout_ref, gpe_ref, wpg_ref, p_ref, wpp_ref, *refs):
    o_ref, act_ref = refs[n_cast], refs[-1]
    _cast_on_the_side(refs[:n_cast], refs[n_cast + 1:-1])
    for rows, h in _swiglu_half_step(x_ref, g1_ref, win_ref, wout_ref, act_ref):
        t = _rms(h, gpe_ref[...]).astype(BF16)
        gate = jax.nn.sigmoid(_dot(t, wpg_ref[...]))
        pe = _dot(p_ref[rows, :].astype(BF16), wpp_ref[...])
        o_ref[rows, :] = h + pe * gate


def _mix_kernel(tiles_per_seq, n_cast,
                h_ref, kp_ref, kc_ref, kn_ref, vp_ref, vc_ref, vn_ref, cp_ref, cc_ref, cn_ref,
                gmix_ref, win_ref, qgain_ref, ones_ref, bias_ref, sink_ref,
                cw_ref, cb_ref, lng_ref, lnb_ref, wao_ref, wco_ref, wo_ref, *refs):
    o_ref = refs[n_cast]
    kext_ref, vext_ref, cext_ref, conv_ref, y_ref, cact_ref, gate_ref = refs[2 * n_cast + 1:]
    _cast_on_the_side(refs[:n_cast], refs[n_cast + 1:2 * n_cast + 1])
    tm, d = h_ref.shape
    gate_col0 = win_ref.shape[1] - gate_ref.shape[1]
    n_blocks = tm // BLOCK
    tile = pl.program_id(0) % tiles_per_seq
    is_first = tile == 0
    is_last = tile == tiles_per_seq - 1

    h = h_ref[...]
    u = _rms(h, gmix_ref[...]).astype(BF16)
    q = _dot(u, win_ref[:, :Q_DIM])
    ss = _group_sumsq(q, ones_ref[...])
    qn = (q * lax.rsqrt(ss * (1.0 / HEAD_DIM) + 1e-6) * qgain_ref[...]).astype(BF16)

    lane_groups = CONV_DIM // LANES
    seg = conv_ref.shape[1] // SUBLANES
    assert seg % 2 == 1 and seg * SUBLANES >= tm and seg % CONV_TILES == 0
    for g in range(lane_groups):
        lanes = slice(g * LANES, (g + 1) * LANES)
        cext_ref[g, :CONV_HALO] = jnp.where(is_first, 0.0, cp_ref[:, lanes])
        cext_ref[g, CONV_HALO:CONV_HALO + tm] = cc_ref[:, lanes]
        cext_ref[g, CONV_HALO + tm:] = jnp.where(is_last, 0.0, cn_ref[:, lanes])
    first_tap = CONV_HALO - CONV_PAD

    def conv_block(g, j0):
        lanes = slice(g * LANES, (g + 1) * LANES)
        windows = {}

        def window(j):
            if j not in windows:
                windows[j] = cext_ref[g, pl.ds(first_tap + j, SUBLANES, stride=seg), :]
            return windows[j]

        accs = [jnp.zeros((SUBLANES, LANES), F32)] * CONV_TILES
        for w in range(CONV_WIDTH):
            tap = cw_ref[w:w + 1, lanes]
            accs = [acc + window(j0 + t + w) * tap for t, acc in enumerate(accs)]
        for t, acc in enumerate(accs):
            conv_ref[g, pl.ds(j0 + t, SUBLANES, stride=seg), :] = acc + cb_ref[:, lanes]

    for g in range(lane_groups):
        for j0 in range(0, seg, CONV_TILES):
            conv_block(g, j0)
    for r0 in range(0, tm, CONV_ROWS):
        c = jnp.concatenate([conv_ref[g, r0:r0 + CONV_ROWS, :] for g in range(lane_groups)], axis=1)
        mu = jnp.mean(c, axis=-1, keepdims=True)
        var = jnp.mean(jnp.square(c - mu), axis=-1, keepdims=True)
        c = (c - mu) * lax.rsqrt(var + 1e-5) * lng_ref[...] + lnb_ref[...]
        cact_ref[r0:r0 + CONV_ROWS, :] = (c * jax.nn.sigmoid(c)).astype(BF16)
    y_conv = _dot(cact_ref[...], wco_ref[...])

    n_chunks = gate_ref.shape[1] // GATE_CHUNK

    def gate_chunk(ci):
        g0 = ci * GATE_CHUNK
        gate_ref[:, g0:g0 + GATE_CHUNK] = jax.nn.sigmoid(
            _dot(u, win_ref[:, gate_col0 + g0:gate_col0 + g0 + GATE_CHUNK]))

    kext_ref[:BLOCK] = kp_ref[...]
    kext_ref[BLOCK:BLOCK + tm] = kc_ref[...]
    kext_ref[BLOCK + tm:] = kn_ref[...]
    vext_ref[:, :BLOCK] = vp_ref[...]
    vext_ref[:, BLOCK:BLOCK + tm] = vc_ref[...]
    vext_ref[:, BLOCK + tm:] = vn_ref[...]

    def scores(r, kv):
        if r == 0:
            table = jnp.where(is_first, 0, 1)
        elif r == n_blocks - 1:
            table = jnp.where(is_last, 2, 1)
        else:
            table = 1
        rows = slice(r * BLOCK, (r + 1) * BLOCK)
        q4 = jnp.concatenate(
            [qn[rows, (kv * GROUP + j) * HEAD_DIM:(kv * GROUP + j + 1) * HEAD_DIM]
             for j in range(GROUP)], axis=0)
        kw = kext_ref[r * BLOCK:(r + 3) * BLOCK, kv * HEAD_DIM:(kv + 1) * HEAD_DIM]
        return lax.dot_general(kw, q4, (((1,), (1,)), ((), ())),
                               preferred_element_type=F32) + bias_ref[table, kv]

    def softmax(s, kv):
        m = jnp.max(s, axis=0, keepdims=True)
        e = jnp.exp2(s - m)
        denom = jnp.sum(e, axis=0, keepdims=True) + jnp.exp2(sink_ref[kv] - m)
        return e.astype(BF16), 1.0 / denom

    def weighted_values(e, inv_denom, r, kv):
        vwt = vext_ref[kv * HEAD_DIM:(kv + 1) * HEAD_DIM, r * BLOCK:(r + 3) * BLOCK]
        o = _dot(vwt, e) * inv_denom
        for pair in range(GROUP // 2):
            two = jnp.concatenate([o[:, (2 * pair) * BLOCK:(2 * pair + 1) * BLOCK],
                                   o[:, (2 * pair + 1) * BLOCK:(2 * pair + 2) * BLOCK]], axis=0)
            col = (kv * GROUP + 2 * pair) * HEAD_DIM
            y_ref[r * BLOCK:(r + 1) * BLOCK, col:col + 2 * HEAD_DIM] = two.T.astype(BF16)

    units = [(r, kv) for r in range(n_blocks) for kv in range(KV_HEADS)]
    assert n_chunks <= len(units) + 2
    s_of, p_of = {}, {}
    for t in range(len(units) + 2):
        if t < len(units):
            s_of[t] = scores(*units[t])
        if t < n_chunks:
            gate_chunk(t)
        if 0 <= t - 1 < len(units):
            p_of[t - 1] = softmax(s_of.pop(t - 1), units[t - 1][1])
        if 0 <= t - 2 < len(units):
            weighted_values(*p_of.pop(t - 2), *units[t - 2])
    y_attn = _dot(y_ref[...], wao_ref[...])

    merged = gate_ref[:, :d] * y_attn + gate_ref[:, d:] * y_conv
    o_ref[...] = h + _dot(merged.astype(BF16), wo_ref[...])


def _bias_tables(rel_bias):
    half = NUM_BUCKETS // 2
    max_exact = half // 2
    period = 4 * BLOCK
    k = np.arange(period)
    rel = np.where(k < 3 * BLOCK, k - BLOCK, k - BLOCK - period)
    n = np.abs(rel)
    nf = np.maximum(n, 1).astype(np.float64)
    large = max_exact + (np.log(nf / max_exact) / math.log(MAX_DISTANCE / max_exact)
                         * (half - max_exact)).astype(np.int32)
    buckets = np.where(rel > 0, half, 0) + np.where(n < max_exact, n, np.minimum(large, half - 1))
    vec = jnp.where((n <= BLOCK)[None, :], rel_bias.astype(F32)[buckets].T, NEG_INF)
    skew = jnp.tile(vec, (1, BLOCK))[:, :BLOCK * (period - 1)].reshape(N_HEADS, BLOCK, period - 1)
    bias = skew[:, :, :3 * BLOCK].reshape(KV_HEADS, GROUP * BLOCK, 3 * BLOCK)
    col = np.arange(3 * BLOCK)
    edge = np.stack([col >= BLOCK, col >= 0, col < 2 * BLOCK])
    tables = jnp.where(edge[:, None, None, :], bias[None], NEG_INF)
    return jnp.swapaxes(tables, -1, -2) * LOG2_E


def _blockdiag_ones(n):
    idx = np.arange(n) // HEAD_DIM
    return jnp.asarray(idx[:, None] == idx[None, :], dtype=BF16)


def _whole(arr):
    zeros = (0,) * arr.ndim
    return pl.BlockSpec(arr.shape, lambda i: zeros, pipeline_mode=pl.Buffered(1))


def _layer(arr, layer):
    index = (layer,) + (0,) * (arr.ndim - 1)
    return pl.BlockSpec((None,) + arr.shape[1:], lambda i: index, pipeline_mode=pl.Buffered(1))


def _rows(width, tm=TILE_M):
    return pl.BlockSpec((tm, width), lambda i: (i, 0))


def _cast_stream(w, layer, n_steps):
    rows, cols = w.shape[1:]
    chunks = next(c for c in range(n_steps, 0, -1)
                  if n_steps % c == 0 and rows % c == 0 and (rows // c) % BF16_ROWS == 0)
    per_chunk, block = n_steps // chunks, rows // chunks
    return (pl.BlockSpec((None, block, cols), lambda i: (layer, i // per_chunk, 0)),
            pl.BlockSpec((block, cols), lambda i: (i // per_chunk, 0)),
            jax.ShapeDtypeStruct((rows, cols), BF16))


_PARAMS = pltpu.CompilerParams(dimension_semantics=("arbitrary",),
                               vmem_limit_bytes=VMEM_LIMIT_BYTES)


def _ffn_kvc(layer, x, g1, win, wout, gmix, wmix, kg, ones):
    m, d = x.shape
    d_ff = wout.shape[0]
    tm = TILE_M_FFN
    lay = lambda a: _layer(a, layer)
    return pl.pallas_call(
        _ffn_kvc_kernel,
        grid=(m // tm,),
        in_specs=[_rows(d, tm), lay(g1), _whole(win), _whole(wout), lay(gmix),
                  pl.BlockSpec((d, QKVC_COLS), lambda i: (0, 0), pipeline_mode=pl.Buffered(1)),
                  lay(kg), _whole(ones)],
        out_specs=[_rows(d, tm), _rows(KV_DIM, tm), pl.BlockSpec((KV_DIM, tm), lambda i: (0, i)),
                   _rows(CONV_DIM, tm)],
        out_shape=[jax.ShapeDtypeStruct((m, d), F32), jax.ShapeDtypeStruct((m, KV_DIM), BF16),
                   jax.ShapeDtypeStruct((KV_DIM, m), BF16), jax.ShapeDtypeStruct((m, CONV_DIM), F32)],
        scratch_shapes=[pltpu.VMEM((tm, d_ff), BF16)],
        compiler_params=_PARAMS,
        name="ffn_kvc",
    )(x, g1, win, wout, gmix, wmix, kg, ones)


def _ffn_ple(layer, x, g1, win, wout, gpe, wpg, p, wpp, cast_next=()):
    m, d = x.shape
    d_ff = wout.shape[0]
    tm = TILE_M_FFN
    lay = lambda a: _layer(a, layer)
    p_spec = pl.BlockSpec((None, tm, p.shape[2]), lambda i: (layer, i, 0))
    casts = [_cast_stream(w, layer + 1, m // tm) for w in cast_next]
    return pl.pallas_call(
        functools.partial(_ffn_ple_kernel, len(casts)),
        grid=(m // tm,),
        in_specs=[_rows(d, tm), lay(g1), _whole(win), _whole(wout), lay(gpe), _whole(wpg), p_spec,
                  _whole(wpp)] + [c[0] for c in casts],
        out_specs=[_rows(d, tm)] + [c[1] for c in casts],
        out_shape=[jax.ShapeDtypeStruct((m, d), F32)] + [c[2] for c in casts],
        scratch_shapes=[pltpu.VMEM((tm, d_ff), BF16)],
        compiler_params=_PARAMS,
        name="ffn_ple",
    )(x, g1, win, wout, gpe, wpg, p, wpp, *cast_next)


def _mix(layer, h, k, v, c, seq, gmix, wmix, qgain, ones, bias, sink, cw, cb, lng, lnb,
         wao, wco, wo, cast_here, cast_next):
    m, d = h.shape
    tm = TILE_M
    tiles_per_seq = seq // tm
    kv_blocks = tm // BLOCK
    last_kv = m // BLOCK - 1
    seg = tm // SUBLANES + 1
    reach = SUBLANES * seg - tm + CONV_PAD
    next_halo = next(n for n in (16, 32, 64, 128) if n >= reach)
    assert tm % next_halo == 0 and tm % CONV_HALO == 0

    def halo(rows, width, stride, last, side):
        if side < 0:
            return pl.BlockSpec((rows, width), lambda i: (jnp.maximum(i * stride - 1, 0), 0))
        return pl.BlockSpec((rows, width), lambda i: (jnp.minimum((i + 1) * stride, last), 0))

    k_specs = [halo(BLOCK, KV_DIM, kv_blocks, last_kv, -1), _rows(KV_DIM),
               halo(BLOCK, KV_DIM, kv_blocks, last_kv, +1)]
    v_specs = [pl.BlockSpec((KV_DIM, BLOCK), lambda i: (0, jnp.maximum(i * kv_blocks - 1, 0))),
               pl.BlockSpec((KV_DIM, tm), lambda i: (0, i)),
               pl.BlockSpec((KV_DIM, BLOCK), lambda i: (0, jnp.minimum((i + 1) * kv_blocks, last_kv)))]
    c_specs = [halo(CONV_HALO, CONV_DIM, tm // CONV_HALO, m // CONV_HALO - 1, -1), _rows(CONV_DIM),
               halo(next_halo, CONV_DIM, tm // next_halo, m // next_halo - 1, +1)]
    lay = lambda a: _layer(a, layer)
    casts = ([_cast_stream(w, layer, m // tm) for w in cast_here]
             + [_cast_stream(w, layer + 1, m // tm) for w in cast_next])
    return pl.pallas_call(
        functools.partial(_mix_kernel, tiles_per_seq, len(casts)),
        grid=(m // tm,),
        in_specs=[_rows(d)] + k_specs + v_specs + c_specs + [
            lay(gmix), _whole(wmix), lay(qgain), _whole(ones), _whole(bias), lay(sink),
            lay(cw), lay(cb), lay(lng), lay(lnb), _whole(wao), _whole(wco), _whole(wo)]
                 + [c[0] for c in casts],
        out_specs=[_rows(d)] + [c[1] for c in casts],
        out_shape=[jax.ShapeDtypeStruct((m, d), F32)] + [c[2] for c in casts],
        scratch_shapes=[pltpu.VMEM((tm + 2 * BLOCK, KV_DIM), BF16),
                        pltpu.VMEM((KV_DIM, tm + 2 * BLOCK), BF16),
                        pltpu.VMEM((CONV_DIM // LANES, CONV_HALO + tm + next_halo, LANES), F32),
                        pltpu.VMEM((CONV_DIM // LANES, SUBLANES * seg, LANES), F32),
                        pltpu.VMEM((tm, Q_DIM), BF16),
                        pltpu.VMEM((tm, CONV_DIM), BF16),
                        pltpu.VMEM((tm, 2 * d), F32)],
        compiler_params=_PARAMS,
        name="mix",
    )(h, k, k, k, v, v, v, c, c, c, gmix, wmix, qgain, ones, bias, sink, cw, cb, lng, lnb,
      wao, wco, wo, *cast_here, *cast_next)


def kernel(x, p, rel_bias, norm_ffn1, w_ffn1_in, w_ffn1_out, norm_mix, w_in, q_norm, k_norm, sink, conv_w, conv_b, conv_ln_g, conv_ln_b, w_attn_out, w_conv_out, w_o, norm_ffn2, w_ffn2_in, w_ffn2_out, norm_pe, w_pe_gate, w_pe_proj):
    batch, seq, d = x.shape
    depth = p.shape[0]
    m = batch * seq
    assert seq % TILE_M == 0 and TILE_M % BLOCK == 0 and seq // BLOCK >= 2
    assert m % TILE_M_FFN == 0 and TILE_M_FFN % FFN_ROWS == 0

    mix_weights = (w_in, w_attn_out, w_conv_out, w_o, w_pe_gate, w_pe_proj)
    w1 = [w_ffn1_in[0].astype(BF16), w_ffn1_out[0].astype(BF16)]
    w_mix, w_ao, w_co, w_oo, w_pg, w_pp = [w[0].astype(BF16) for w in mix_weights]

    bias = _bias_tables(rel_bias)
    ones_q, ones_k = _blockdiag_ones(Q_DIM), _blockdiag_ones(KV_DIM)
    vec = lambda a: a[:, None, :]
    q_gain = vec(jnp.tile(q_norm, (1, N_HEADS)) * (HEAD_DIM ** -0.5 * LOG2_E))
    k_gain = vec(jnp.tile(k_norm, (1, KV_HEADS)))
    sink_rows = jnp.repeat(sink.reshape(depth, KV_HEADS, 1, GROUP), BLOCK, axis=-1) * LOG2_E
    g_ffn1, g_mix, g_ffn2, g_pe = vec(norm_ffn1), vec(norm_mix), vec(norm_ffn2), vec(norm_pe)
    cb, lng, lnb = vec(conv_b), vec(conv_ln_g), vec(conv_ln_b)

    xf = x.reshape(m, d)
    pf = p.reshape(depth, m, p.shape[-1])
    for i in range(depth):
        last = i + 1 == depth
        h, k, v, c = _ffn_kvc(i, xf, g_ffn1, w1[0], w1[1], g_mix, w_mix, k_gain, ones_k)
        h, w2_in, w2_out, *next_mix = _mix(
            i, h, k, v, c, seq, g_mix, w_mix, q_gain, ones_q, bias, sink_rows, conv_w, cb, lng, lnb,
            w_ao, w_co, w_oo, (w_ffn2_in, w_ffn2_out), () if last else mix_weights)
        xf, *w1 = _ffn_ple(i, h, g_ffn2, w2_in, w2_out, g_pe, w_pg, pf, w_pp,
                           () if last else (w_ffn1_in, w_ffn1_out))
        if not last:
            w_mix, w_ao, w_co, w_oo, w_pg, w_pp = next_mix
    return xf.reshape(batch, seq, d)
```
